```python
import jax, jax.numpy as jnp
from jax import lax
import numpy as np

D_MODEL = 2048
BATCH = 1
SEQ = 8192
DEPTH = 4
DEC_BATCH = 8
DEC_SEQ = 2048
PAST_LEN = 128

HEAD_DIM = 128
N_Q_HEADS = 8
N_KV_HEADS = 2
GROUP = N_Q_HEADS // N_KV_HEADS
ATTN_WIDTH = N_Q_HEADS * HEAD_DIM
KV_WIDTH = N_KV_HEADS * HEAD_DIM
CONV_WIDTH = D_MODEL // 2
CONV_K = 3
WINDOW = 128
BLOCK = 128
ROT_DIM = HEAD_DIM // 4
ROPE_THETA = 500000.0
D_FF = 4 * D_MODEL
N_BRANCH = 2
RMS_EPS = 1e-6
PROJ_WIDTH = ATTN_WIDTH + 2 * KV_WIDTH + 3 * CONV_WIDTH + N_BRANCH * D_MODEL
SPLITS = [ATTN_WIDTH,
          ATTN_WIDTH + KV_WIDTH,
          ATTN_WIDTH + 2 * KV_WIDTH,
          ATTN_WIDTH + 2 * KV_WIDTH + CONV_WIDTH,
          ATTN_WIDTH + 2 * KV_WIDTH + 2 * CONV_WIDTH,
          ATTN_WIDTH + 2 * KV_WIDTH + 3 * CONV_WIDTH]

kernel_name = "hybrid_swa_shortconv_encoder"


def rmsnorm(x, g):
    x32 = x.astype(jnp.float32)
    y = x32 * lax.rsqrt(jnp.mean(x32 * x32, axis=-1, keepdims=True) + RMS_EPS)
    return (y * g.astype(jnp.float32)).astype(x.dtype)


def partial_rope(x, positions):
    inv_freq = ROPE_THETA ** (-jnp.arange(0, ROT_DIM, 2, dtype=jnp.float32) / ROT_DIM)
    ang = positions.astype(jnp.float32)[:, None] * inv_freq[None, :]
    cos = jnp.cos(ang)[None, :, None, :]
    sin = jnp.sin(ang)[None, :, None, :]
    xr = x[..., :ROT_DIM].astype(jnp.float32)
    x1, x2 = xr[..., :ROT_DIM // 2], xr[..., ROT_DIM // 2:]
    rot = jnp.concatenate([x1 * cos - x2 * sin, x2 * cos + x1 * sin], axis=-1).astype(x.dtype)
    return jnp.concatenate([rot, x[..., ROT_DIM:]], axis=-1)


def banded_sink_attention(q, k, v, sink):
    b, s = q.shape[0], q.shape[1]
    nb = s // BLOCK
    qb = q.reshape(b, nb, BLOCK, N_KV_HEADS, GROUP, HEAD_DIM)
    pad = ((0, 0), (BLOCK, BLOCK), (0, 0), (0, 0))
    kp = jnp.pad(k, pad).reshape(b, nb + 2, BLOCK, N_KV_HEADS, HEAD_DIM)
    vp = jnp.pad(v, pad).reshape(b, nb + 2, BLOCK, N_KV_HEADS, HEAD_DIM)
    kb = jnp.concatenate([kp[:, :-2], kp[:, 1:-1], kp[:, 2:]], axis=2)
    vb = jnp.concatenate([vp[:, :-2], vp[:, 1:-1], vp[:, 2:]], axis=2)
    scores = jnp.einsum('bnqkgd,bnskd->bnkgqs', qb, kb).astype(jnp.float32) * (HEAD_DIM ** -0.5)
    blk = jnp.arange(nb)
    qpos = blk[:, None] * BLOCK + jnp.arange(BLOCK)[None, :]
    kpos = (blk[:, None] - 1) * BLOCK + jnp.arange(3 * BLOCK)[None, :]
    rel = kpos[:, None, :] - qpos[:, :, None]
    valid = (jnp.abs(rel) <= WINDOW) & (kpos[:, None, :] >= 0) & (kpos[:, None, :] < s)
    scores = jnp.where(valid[None, :, None, None], scores, -1e30)
    sink_col = jnp.broadcast_to(sink.astype(jnp.float32).reshape(1, 1, N_KV_HEADS, GROUP, 1, 1),
                                scores.shape[:-1] + (1,))
    probs = jax.nn.softmax(jnp.concatenate([scores, sink_col], axis=-1), axis=-1)[..., :-1]
    out = jnp.einsum('bnkgqs,bnskd->bnqkgd', probs.astype(v.dtype), vb)
    return out.reshape(b, s, ATTN_WIDTH)


def short_conv(u, w):
    up = jnp.pad(u, ((0, 0), (1, 1), (0, 0)))
    return up[:, :-2] * w[0] + up[:, 1:-1] * w[1] + up[:, 2:] * w[2]


def encoder_layer(x, g_pre_mix, w_in, b_gate, w_sink, w_conv, w_attn_out, w_conv_out, w_out,
                  g_post_mix, g_pre_mlp, w_mlp_in, w_mlp_out, g_post_mlp):
    b, s, _ = x.shape
    h = rmsnorm(x, g_pre_mix)
    proj = h @ w_in
    q, k, v, bg, cg, xc, gates = jnp.split(proj, SPLITS, axis=-1)
    pos = jnp.arange(s)
    q = partial_rope(q.reshape(b, s, N_Q_HEADS, HEAD_DIM), pos)
    k = partial_rope(k.reshape(b, s, N_KV_HEADS, HEAD_DIM), pos)
    v = v.reshape(b, s, N_KV_HEADS, HEAD_DIM)
    o_attn = banded_sink_attention(q, k, v, w_sink) @ w_attn_out
    o_conv = (bg * short_conv(cg * xc, w_conv)) @ w_conv_out
    ga, gc = jnp.split(gates + b_gate, 2, axis=-1)
    merged = jax.nn.sigmoid(ga) * o_attn + jax.nn.sigmoid(gc) * o_conv
    x = x + rmsnorm(merged @ w_out, g_post_mix)
    hm = rmsnorm(x, g_pre_mlp)
    f = jnp.square(jax.nn.relu(hm @ w_mlp_in)) @ w_mlp_out
    return x + rmsnorm(f, g_post_mlp)


def trunk(x, g_pre_mix, w_in, b_gate, w_sink, w_conv, w_attn_out, w_conv_out, w_out,
          g_post_mix, g_pre_mlp, w_mlp_in, w_mlp_out, g_post_mlp):
    for l in range(DEPTH):
        x = encoder_layer(x, g_pre_mix[l], w_in[l], b_gate[l], w_sink[l], w_conv[l], w_attn_out[l],
                          w_conv_out[l], w_out[l], g_post_mix[l], g_pre_mlp[l], w_mlp_in[l],
                          w_mlp_out[l], g_post_mlp[l])
    return x


def setup_inputs(seed: int = 0) -> dict:
    key = jax.random.key(seed)
    ks = jax.random.split(key, 16)
    f32 = jnp.float32

    def nrm(k, shape, scale):
        return jax.random.normal(k, shape, f32) * scale

    def gain(k):
        return 1.0 + 0.02 * jax.random.normal(k, (DEPTH, D_MODEL), f32)

    return {
        "x_prompt": jax.random.normal(ks[0], (BATCH, SEQ, D_MODEL), f32),
        "x_sample": jax.random.normal(ks[1], (DEC_BATCH, DEC_SEQ, D_MODEL), f32),
        "g_pre_mix": gain(ks[2]),
        "w_in": nrm(ks[3], (DEPTH, D_MODEL, PROJ_WIDTH), D_MODEL ** -0.5),
        "b_gate": nrm(ks[4], (DEPTH, N_BRANCH * D_MODEL), 0.01),
        "w_sink": nrm(ks[5], (DEPTH, N_Q_HEADS), 0.5),
        "w_conv": nrm(ks[6], (DEPTH, CONV_K, CONV_WIDTH), CONV_K ** -0.5),
        "w_attn_out": nrm(ks[7], (DEPTH, ATTN_WIDTH, D_MODEL), ATTN_WIDTH ** -0.5),
        "w_conv_out": nrm(ks[8], (DEPTH, CONV_WIDTH, D_MODEL), CONV_WIDTH ** -0.5),
        "w_out": nrm(ks[9], (DEPTH, D_MODEL, D_MODEL), D_MODEL ** -0.5),
        "g_post_mix": gain(ks[10]),
        "g_pre_mlp": gain(ks[11]),
        "w_mlp_in": nrm(ks[12], (DEPTH, D_MODEL, D_FF), D_MODEL ** -0.5),
        "w_mlp_out": nrm(ks[13], (DEPTH, D_FF, D_MODEL), D_FF ** -0.5),
        "g_post_mlp": gain(ks[14]),
    }


def reference(x_prompt, x_sample, g_pre_mix, w_in, b_gate, w_sink, w_conv, w_attn_out, w_conv_out,
              w_out, g_post_mix, g_pre_mlp, w_mlp_in, w_mlp_out, g_post_mlp):
    y_prompt = trunk(x_prompt, g_pre_mix, w_in, b_gate, w_sink, w_conv, w_attn_out, w_conv_out,
                     w_out, g_post_mix, g_pre_mlp, w_mlp_in, w_mlp_out, g_post_mlp)
    y_sample = trunk(x_sample, g_pre_mix, w_in, b_gate, w_sink, w_conv, w_attn_out, w_conv_out,
                     w_out, g_post_mix, g_pre_mlp, w_mlp_in, w_mlp_out, g_post_mlp)
    return (y_prompt, y_sample)
```

```python
import functools
from typing import NamedTuple

import numpy as np
import jax
import jax.numpy as jnp
from jax import lax
from jax.experimental import pallas as pl
from jax.experimental.pallas import tpu as pltpu

F32 = jnp.float32
BF16 = jnp.bfloat16

HEAD_DIM = 128
N_Q_HEADS = 8
N_KV_HEADS = 2
GROUP = N_Q_HEADS // N_KV_HEADS
ATTN_WIDTH = N_Q_HEADS * HEAD_DIM
KV_WIDTH = N_KV_HEADS * HEAD_DIM
BLOCK = 128
ROT_DIM = HEAD_DIM // 4
ROT_HALF = ROT_DIM // 2
ROPE_THETA = 500000.0
RMS_EPS = 1e-6
NEG_INF = -1e30
SUBLANES = 8
V7X_VMEM_LIMIT_BYTES = 60000 * 1024


class Tiles(NamedTuple):
    proj_rows: int
    proj_cols: int
    attn_rows: int
    mix_rows: int
    mix_cols: int
    mlp_rows: int
    mlp_cols: int


V7X_TILES = Tiles(proj_rows=1024, proj_cols=1024, attn_rows=1024, mix_rows=512,
                  mix_cols=512, mlp_rows=1024, mlp_cols=512)


def _rmsnorm(x, g):
    return x * lax.rsqrt(jnp.mean(x * x, axis=-1, keepdims=True) + RMS_EPS) * g


def _rope_head(xh, cos, sin, lane):
    partner = jnp.where(lane < ROT_HALF,
                        pltpu.roll(xh, HEAD_DIM - ROT_HALF, 1),
                        pltpu.roll(xh, ROT_HALF, 1))
    return jnp.where(lane < ROT_DIM, xh * cos + partner * sin, xh)


def _in_proj_kernel(pos_blk_ref, x_ref, g_ref, w_ref, wkv_ref, cos_ref, sin_ref,
                    proj_ref, kv_ref, h_ref):
    del pos_blk_ref
    j = pl.program_id(1)
    rows = x_ref.shape[0]
    lane = lax.broadcasted_iota(jnp.int32, (rows, HEAD_DIM), 1)

    @pl.when(j == 0)
    def _():
        h_ref[...] = _rmsnorm(x_ref[...], g_ref[...]).astype(BF16)
        cos = cos_ref[...]
        sin = sin_ref[...]
        kv = jnp.dot(h_ref[...], wkv_ref[...], preferred_element_type=F32)
        for hd in range(N_KV_HEADS):
            sl = slice(hd * HEAD_DIM, (hd + 1) * HEAD_DIM)
            kv_ref[:, sl] = _rope_head(kv[:, sl], cos, sin, lane).astype(BF16)
        kv_ref[:, KV_WIDTH:] = kv[:, KV_WIDTH:].astype(BF16)
        q = jnp.dot(h_ref[...], w_ref[...], preferred_element_type=F32)
        for hd in range(q.shape[1] // HEAD_DIM):
            sl = slice(hd * HEAD_DIM, (hd + 1) * HEAD_DIM)
            proj_ref[:, sl] = _rope_head(q[:, sl], cos, sin, lane).astype(BF16)

    @pl.when(j != 0)
    def _():
        proj_ref[...] = jnp.dot(h_ref[...], w_ref[...],
                                preferred_element_type=F32).astype(BF16)


def _in_proj(x, g, w_main, w_kv, cos_t, sin_t, pos_blk, tiles):
    t, d = x.shape
    n_main = w_main.shape[1]
    tm, tn = tiles.proj_rows, tiles.proj_cols
    assert tn == ATTN_WIDTH and n_main % tn == 0 and t % tm == 0
    grid_spec = pltpu.PrefetchScalarGridSpec(
        num_scalar_prefetch=1,
        grid=(t // tm, n_main // tn),
        in_specs=[
            pl.BlockSpec((tm, d), lambda i, j, p: (i, 0)),
            pl.BlockSpec((1, d), lambda i, j, p: (0, 0)),
            pl.BlockSpec((d, tn), lambda i, j, p: (0, j)),
            pl.BlockSpec((d, 2 * KV_WIDTH), lambda i, j, p: (0, 0)),
            pl.BlockSpec((tm, HEAD_DIM), lambda i, j, p: (p[i], 0)),
            pl.BlockSpec((tm, HEAD_DIM), lambda i, j, p: (p[i], 0)),
        ],
        out_specs=[
            pl.BlockSpec((tm, tn), lambda i, j, p: (i, j)),
            pl.BlockSpec((tm, 2 * KV_WIDTH), lambda i, j, p: (i, 0)),
        ],
        scratch_shapes=[pltpu.VMEM((tm, d), BF16)],
    )
    return pl.pallas_call(
        _in_proj_kernel,
        grid_spec=grid_spec,
        out_shape=[jax.ShapeDtypeStruct((t, n_main), BF16),
                   jax.ShapeDtypeStruct((t, 2 * KV_WIDTH), BF16)],
        compiler_params=pltpu.CompilerParams(
            dimension_semantics=("arbitrary", "arbitrary"),
            vmem_limit_bytes=V7X_VMEM_LIMIT_BYTES),
        name="in_proj",
    )(pos_blk, x, g, w_main, w_kv, cos_t, sin_t)


def _attn_kernel(edge_ref, sink_ref, q_ref, k_ref, v_ref, kp_ref, vp_ref, kn_ref, vn_ref,
                 o_ref, kbuf, vbuf):
    i = pl.program_id(0)
    rows = q_ref.shape[0]
    nblk = rows // BLOCK
    has_prev = edge_ref[0, i]
    has_next = edge_ref[1, i]
    kbuf[0:BLOCK, :] = kp_ref[...]
    kbuf[BLOCK:BLOCK + rows, :] = k_ref[...]
    kbuf[BLOCK + rows:, :] = kn_ref[...]
    vbuf[0:BLOCK, :] = vp_ref[...]
    vbuf[BLOCK:BLOCK + rows, :] = v_ref[...]
    vbuf[BLOCK + rows:, :] = vn_ref[...]

    r_io = lax.broadcasted_iota(jnp.int32, (BLOCK, 3 * BLOCK), 0)
    s_io = lax.broadcasted_iota(jnp.int32, (BLOCK, 3 * BLOCK), 1)
    scale = HEAD_DIM ** -0.5

    def block_body(b, carry):
        lo = jnp.where(jnp.logical_or(b > 0, has_prev == 1), 0, BLOCK)
        hi = jnp.where(jnp.logical_or(b < nblk - 1, has_next == 1), 3 * BLOCK, 2 * BLOCK)
        valid = (s_io >= jnp.maximum(r_io, lo)) & (s_io <= r_io + 2 * BLOCK) & (s_io < hi)
        q0 = pl.multiple_of(b * BLOCK, BLOCK)
        for c in range(N_KV_HEADS):
            csl = slice(c * HEAD_DIM, (c + 1) * HEAD_DIM)
            kb = kbuf[pl.ds(q0, 3 * BLOCK), csl]
            vb = vbuf[pl.ds(q0, 3 * BLOCK), csl]
            qs = jnp.concatenate(
                [q_ref[pl.ds(q0, BLOCK), (c * GROUP + g) * HEAD_DIM:(c * GROUP + g + 1) * HEAD_DIM]
                 for g in range(GROUP)], axis=0)
            s = lax.dot_general(qs, kb, (((1,), (1,)), ((), ())),
                                preferred_element_type=F32) * scale
            ps, invs = [], []
            for g in range(GROUP):
                sink = sink_ref[c * GROUP + g]
                sg = jnp.where(valid, s[g * BLOCK:(g + 1) * BLOCK], NEG_INF)
                m = jnp.maximum(jnp.max(sg, axis=-1, keepdims=True), sink)
                p = jnp.exp(sg - m)
                den = jnp.sum(p, axis=-1, keepdims=True) + jnp.exp(sink - m)
                ps.append(p.astype(BF16))
                invs.append(1.0 / den)
            o = jnp.dot(jnp.concatenate(ps, axis=0), vb, preferred_element_type=F32)
            for g in range(GROUP):
                hsl = slice((c * GROUP + g) * HEAD_DIM, (c * GROUP + g + 1) * HEAD_DIM)
                o_ref[pl.ds(q0, BLOCK), hsl] = (o[g * BLOCK:(g + 1) * BLOCK] * invs[g]).astype(BF16)
        return carry

    lax.fori_loop(0, nblk, block_body, 0)


def _attn(proj, kv, sink, edges, tiles):
    t = proj.shape[0]
    tq = tiles.attn_rows
    assert t % tq == 0 and tq % BLOCK == 0
    per = tq // BLOCK
    last_blk = t // BLOCK - 1
    prev_map = lambda i, e: (jnp.maximum(i * per - 1, 0), 0)
    next_map = lambda i, e: (jnp.minimum((i + 1) * per, last_blk), 0)
    prev_map_v = lambda i, e: (jnp.maximum(i * per - 1, 0), 1)
    next_map_v = lambda i, e: (jnp.minimum((i + 1) * per, last_blk), 1)
    grid_spec = pltpu.PrefetchScalarGridSpec(
        num_scalar_prefetch=1,
        grid=(t // tq,),
        in_specs=[
            pl.BlockSpec(memory_space=pltpu.SMEM),
            pl.BlockSpec((tq, ATTN_WIDTH), lambda i, e: (i, 0)),
            pl.BlockSpec((tq, KV_WIDTH), lambda i, e: (i, 0)),
            pl.BlockSpec((tq, KV_WIDTH), lambda i, e: (i, 1)),
            pl.BlockSpec((BLOCK, KV_WIDTH), prev_map),
            pl.BlockSpec((BLOCK, KV_WIDTH), prev_map_v),
            pl.BlockSpec((BLOCK, KV_WIDTH), next_map),
            pl.BlockSpec((BLOCK, KV_WIDTH), next_map_v),
        ],
        out_specs=pl.BlockSpec((tq, ATTN_WIDTH), lambda i, e: (i, 0)),
        scratch_shapes=[pltpu.VMEM((tq + 2 * BLOCK, KV_WIDTH), BF16),
                        pltpu.VMEM((tq + 2 * BLOCK, KV_WIDTH), BF16)],
    )
    return pl.pallas_call(
        _attn_kernel,
        grid_spec=grid_spec,
        out_shape=jax.ShapeDtypeStruct((t, ATTN_WIDTH), BF16),
        compiler_params=pltpu.CompilerParams(
            dimension_semantics=("arbitrary",),
            vmem_limit_bytes=V7X_VMEM_LIMIT_BYTES),
        name="attn",
    )(edges, sink, proj, kv, kv, kv, kv, kv, kv)


def _mix_out_kernel(edge_ref, attn_ref, b_ref, c_ref, xc_ref, cp_ref, xp_ref, cn_ref, xn_ref,
                    ga_ref, gc_ref, bga_ref, bgc_ref, wconv_ref, wao_ref, wco_ref, wout_ref,
                    x_ref, gpost_ref, o_ref, conv_ref, acc_ref):
    i = pl.program_id(0)
    j = pl.program_id(1)
    rows = x_ref.shape[0]

    @pl.when(j == 0)
    def _():
        u = c_ref[...].astype(F32) * xc_ref[...].astype(F32)
        up = (cp_ref[SUBLANES - 1:SUBLANES, :].astype(F32)
              * xp_ref[SUBLANES - 1:SUBLANES, :].astype(F32))
        un = cn_ref[0:1, :].astype(F32) * xn_ref[0:1, :].astype(F32)
        up = jnp.where(edge_ref[0, i] == 1, up, 0.0)
        un = jnp.where(edge_ref[1, i] == 1, un, 0.0)
        r_io = lax.broadcasted_iota(jnp.int32, u.shape, 0)
        u_prev = jnp.where(r_io == 0, up, pltpu.roll(u, 1, 0))
        u_next = jnp.where(r_io == rows - 1, un, pltpu.roll(u, rows - 1, 0))
        conv = u_prev * wconv_ref[0:1, :] + u * wconv_ref[1:2, :] + u_next * wconv_ref[2:3, :]
        conv_ref[...] = (b_ref[...].astype(F32) * conv).astype(BF16)
        acc_ref[...] = jnp.zeros_like(acc_ref)

    o_a = jnp.dot(attn_ref[...], wao_ref[...], preferred_element_type=F32)
    o_c = jnp.dot(conv_ref[...], wco_ref[...], preferred_element_type=F32)
    merged = (jax.nn.sigmoid(ga_ref[...].astype(F32) + bga_ref[...]) * o_a
              + jax.nn.sigmoid(gc_ref[...].astype(F32) + bgc_ref[...]) * o_c)
    acc_ref[...] += jnp.dot(merged.astype(BF16), wout_ref[...], preferred_element_type=F32)

    @pl.when(j == pl.num_programs(1) - 1)
    def _():
        o_ref[...] = x_ref[...] + _rmsnorm(acc_ref[...], gpost_ref[...])


def _mix_out(x, attn, proj, b_gate, w_conv, w_ao, w_co, w_out, g_post, edges, tiles):
    t, d = x.shape
    cw = w_co.shape[0]
    tm, tc = tiles.mix_rows, tiles.mix_cols
    assert t % tm == 0 and d % tc == 0 and tm % SUBLANES == 0
    assert ATTN_WIDTH == cw
    nj = d // tc
    ga0 = (ATTN_WIDTH + 3 * cw) // tc
    per8 = tm // SUBLANES
    last8 = t // SUBLANES - 1
    prev8 = lambda col: (lambda i, j, e: (jnp.maximum(i * per8 - 1, 0), col))
    next8 = lambda col: (lambda i, j, e: (jnp.minimum((i + 1) * per8, last8), col))
    grid_spec = pltpu.PrefetchScalarGridSpec(
        num_scalar_prefetch=1,
        grid=(t // tm, nj),
        in_specs=[
            pl.BlockSpec((tm, ATTN_WIDTH), lambda i, j, e: (i, 0)),
            pl.BlockSpec((tm, cw), lambda i, j, e: (i, 1)),
            pl.BlockSpec((tm, cw), lambda i, j, e: (i, 2)),
            pl.BlockSpec((tm, cw), lambda i, j, e: (i, 3)),
            pl.BlockSpec((SUBLANES, cw), prev8(2)),
            pl.BlockSpec((SUBLANES, cw), prev8(3)),
            pl.BlockSpec((SUBLANES, cw), next8(2)),
            pl.BlockSpec((SUBLANES, cw), next8(3)),
            pl.BlockSpec((tm, tc), lambda i, j, e: (i, ga0 + j)),
            pl.BlockSpec((tm, tc), lambda i, j, e: (i, ga0 + nj + j)),
            pl.BlockSpec((1, tc), lambda i, j, e: (0, j)),
            pl.BlockSpec((1, tc), lambda i, j, e: (0, nj + j)),
            pl.BlockSpec((3, cw), lambda i, j, e: (0, 0)),
            pl.BlockSpec((ATTN_WIDTH, tc), lambda i, j, e: (0, j)),
            pl.BlockSpec((cw, tc), lambda i, j, e: (0, j)),
            pl.BlockSpec((tc, d), lambda i, j, e: (j, 0)),
            pl.BlockSpec((tm, d), lambda i, j, e: (i, 0)),
            pl.BlockSpec((1, d), lambda i, j, e: (0, 0)),
        ],
        out_specs=pl.BlockSpec((tm, d), lambda i, j, e: (i, 0)),
        scratch_shapes=[pltpu.VMEM((tm, cw), BF16), pltpu.VMEM((tm, d), F32)],
    )
    return pl.pallas_call(
        _mix_out_kernel,
        grid_spec=grid_spec,
        out_shape=jax.ShapeDtypeStruct((t, d), F32),
        compiler_params=pltpu.CompilerParams(
            dimension_semantics=("arbitrary", "arbitrary"),
            vmem_limit_bytes=V7X_VMEM_LIMIT_BYTES),
        name="mix_out",
    )(edges, attn, proj, proj, proj, proj, proj, proj, proj, proj, proj,
      b_gate, b_gate, w_conv, w_ao, w_co, w_out, x, g_post)


def _mlp_kernel(x_ref, gpre_ref, w1_ref, w2_ref, gpost_ref, o_ref, h_ref):
    j = pl.program_id(1)

    @pl.when(j == 0)
    def _():
        h_ref[...] = _rmsnorm(x_ref[...], gpre_ref[...]).astype(BF16)
        o_ref[...] = jnp.zeros_like(o_ref)

    a = jnp.dot(h_ref[...], w1_ref[...], preferred_element_type=F32)
    a = jnp.square(jnp.maximum(a, 0.0)).astype(BF16)
    o_ref[...] += jnp.dot(a, w2_ref[...], preferred_element_type=F32)

    @pl.when(j == pl.num_programs(1) - 1)
    def _():
        o_ref[...] = x_ref[...] + _rmsnorm(o_ref[...], gpost_ref[...])


def _mlp(x, g_pre, w1, w2, g_post, tiles):
    t, d = x.shape
    f = w1.shape[1]
    tm, tf = tiles.mlp_rows, tiles.mlp_cols
    assert t % tm == 0 and f % tf == 0
    return pl.pallas_call(
        _mlp_kernel,
        grid=(t // tm, f // tf),
        in_specs=[
            pl.BlockSpec((tm, d), lambda i, j: (i, 0)),
            pl.BlockSpec((1, d), lambda i, j: (0, 0)),
            pl.BlockSpec((d, tf), lambda i, j: (0, j)),
            pl.BlockSpec((tf, d), lambda i, j: (j, 0)),
            pl.BlockSpec((1, d), lambda i, j: (0, 0)),
        ],
        out_specs=pl.BlockSpec((tm, d), lambda i, j: (i, 0)),
        out_shape=jax.ShapeDtypeStruct((t, d), F32),
        scratch_shapes=[pltpu.VMEM((tm, d), BF16)],
        compiler_params=pltpu.CompilerParams(
            dimension_semantics=("arbitrary", "arbitrary"),
            vmem_limit_bytes=V7X_VMEM_LIMIT_BYTES),
        name="mlp",
    )(x, g_pre, w1, w2, g_post)


def _tile_edges(seq_lens, tile):
    prev, nxt = [], []
    for s in seq_lens:
        assert s % tile == 0
        n = s // tile
        prev += [0] + [1] * (n - 1)
        nxt += [1] * (n - 1) + [0]
    return jnp.asarray(np.array([prev, nxt], dtype=np.int32))


def _tile_pos_blocks(seq_lens, tile):
    out = []
    for s in seq_lens:
        assert s % tile == 0
        out += list(range(s // tile))
    return jnp.asarray(np.array(out, dtype=np.int32))


def _rope_tables(max_len):
    inv_freq = ROPE_THETA ** (-jnp.arange(0, ROT_DIM, 2, dtype=F32) / ROT_DIM)
    ang = jnp.arange(max_len).astype(F32)[:, None] * inv_freq[None, :]
    cos, sin = jnp.cos(ang), jnp.sin(ang)
    rest = HEAD_DIM - ROT_DIM
    cos_t = jnp.concatenate([cos, cos, jnp.ones((max_len, rest), F32)], axis=-1)
    sin_t = jnp.concatenate([-sin, sin, jnp.zeros((max_len, rest), F32)], axis=-1)
    return cos_t, sin_t


def _trunk(groups, params, tiles):
    (g_pre_mix, w_in, b_gate, w_sink, w_conv, w_attn_out, w_conv_out, w_out,
     g_post_mix, g_pre_mlp, w_mlp_in, w_mlp_out, g_post_mlp) = params
    depth, d, _ = w_in.shape
    cw = w_conv.shape[2]
    seq_lens = [g.shape[1] for g in groups for _ in range(g.shape[0])]
    x = jnp.concatenate([g.reshape(-1, d) for g in groups], axis=0)

    cos_t, sin_t = _rope_tables(max(seq_lens))
    pos_blk = _tile_pos_blocks(seq_lens, tiles.proj_rows)
    attn_edges = _tile_edges(seq_lens, tiles.attn_rows)
    mix_edges = _tile_edges(seq_lens, tiles.mix_rows)

    kv0 = ATTN_WIDTH
    rest0 = ATTN_WIDTH + 2 * KV_WIDTH
    for l in range(depth):
        w_main = jnp.concatenate([w_in[l, :, :kv0], w_in[l, :, rest0:]], axis=1).astype(BF16)
        w_kv = w_in[l, :, kv0:rest0].astype(BF16)
        proj, kv = _in_proj(x, g_pre_mix[l][None], w_main, w_kv, cos_t, sin_t, pos_blk, tiles)
        attn = _attn(proj, kv, w_sink[l], attn_edges, tiles)
        x = _mix_out(x, attn, proj, b_gate[l][None], w_conv[l],
                     w_attn_out[l].astype(BF16), w_conv_out[l].astype(BF16),
                     w_out[l].astype(BF16), g_post_mix[l][None], mix_edges, tiles)
        x = _mlp(x, g_pre_mlp[l][None], w_mlp_in[l].astype(BF16), w_mlp_out[l].astype(BF16),
                 g_post_mlp[l][None], tiles)
    del cw

    outs, r0 = [], 0
    for g in groups:
        n = g.shape[0] * g.shape[1]
        outs.append(x[r0:r0 + n].reshape(g.shape))
        r0 += n
    return tuple(outs)


def kernel(x_prompt, x_sample, g_pre_mix, w_in, b_gate, w_sink, w_conv, w_attn_out, w_conv_out,
           w_out, g_post_mix, g_pre_mlp, w_mlp_in, w_mlp_out, g_post_mlp):
    params = (g_pre_mix, w_in, b_gate, w_sink, w_conv, w_attn_out, w_conv_out, w_out,
              g_post_mix, g_pre_mlp, w_mlp_in, w_mlp_out, g_post_mlp)
    return _trunk([x_prompt, x_sample], params, V7X_TILES)
```

```python
from typing import NamedTuple

import numpy as np
import jax
import jax.numpy as jnp
from jax import lax
from jax.experimental import pallas as pl
from jax.experimental.pallas import tpu as pltpu

F32 = jnp.float32
BF16 = jnp.bfloat16

HEAD_DIM = 128
N_Q_HEADS = 8
N_KV_HEADS = 2
GROUP = N_Q_HEADS // N_KV_HEADS
ATTN_WIDTH = N_Q_HEADS * HEAD_DIM
KV_WIDTH = N_KV_HEADS * HEAD_DIM
BLOCK = 128
ROT_DIM = HEAD_DIM // 4
ROT_HALF = ROT_DIM // 2
ROPE_THETA = 500000.0
RMS_EPS = 1e-6
NEG_INF = -1e30
SUBLANES = 8
V7X_VMEM_LIMIT_BYTES = 60000 * 1024


class Tiles(NamedTuple):
    proj_rows: int
    proj_cols: int
    attn_rows: int
    mix_rows: int
    mlp_rows: int
    mlp_cols: int


V7X_TILES = Tiles(proj_rows=1024, proj_cols=1024, attn_rows=1024, mix_rows=256,
                  mlp_rows=1024, mlp_cols=512)


def _rmsnorm(x, g):
    return x * lax.rsqrt(jnp.mean(x * x, axis=-1, keepdims=True) + RMS_EPS) * g


def _rope_head(xh, cos, sin, lane):
    partner = jnp.where(lane < ROT_HALF,
                        pltpu.roll(xh, HEAD_DIM - ROT_HALF, 1),
                        pltpu.roll(xh, ROT_HALF, 1))
    return jnp.where(lane < ROT_DIM, xh * cos + partner * sin, xh)


def _params(semantics):
    return pltpu.CompilerParams(dimension_semantics=semantics,
                                vmem_limit_bytes=V7X_VMEM_LIMIT_BYTES)


def _in_proj_kernel(pos_blk_ref, x_ref, g_ref, w_ref, wkv_ref, cos_ref, sin_ref,
                    proj_ref, kv_ref, h_ref):
    del pos_blk_ref
    j = pl.program_id(1)
    rows = x_ref.shape[0]
    lane = lax.broadcasted_iota(jnp.int32, (rows, HEAD_DIM), 1)

    @pl.when(j == 0)
    def _():
        h_ref[...] = _rmsnorm(x_ref[...], g_ref[...]).astype(BF16)
        cos = cos_ref[...]
        sin = sin_ref[...]
        kv = jnp.dot(h_ref[...], wkv_ref[...], preferred_element_type=F32)
        for hd in range(N_KV_HEADS):
            sl = slice(hd * HEAD_DIM, (hd + 1) * HEAD_DIM)
            kv_ref[:, sl] = _rope_head(kv[:, sl], cos, sin, lane).astype(BF16)
        kv_ref[:, KV_WIDTH:] = kv[:, KV_WIDTH:].astype(BF16)
        q = jnp.dot(h_ref[...], w_ref[...], preferred_element_type=F32)
        for hd in range(q.shape[1] // HEAD_DIM):
            sl = slice(hd * HEAD_DIM, (hd + 1) * HEAD_DIM)
            proj_ref[:, sl] = _rope_head(q[:, sl], cos, sin, lane).astype(BF16)

    @pl.when(j != 0)
    def _():
        proj_ref[...] = jnp.dot(h_ref[...], w_ref[...],
                                preferred_element_type=F32).astype(BF16)


def _in_proj(layer, x, g, w_main, w_kv, cos_t, sin_t, pos_blk, tiles):
    t, d = x.shape
    n_main = w_main.shape[2]
    tm, tn = tiles.proj_rows, tiles.proj_cols
    assert tn == ATTN_WIDTH and n_main % tn == 0 and t % tm == 0
    grid_spec = pltpu.PrefetchScalarGridSpec(
        num_scalar_prefetch=1,
        grid=(t // tm, n_main // tn),
        in_specs=[
            pl.BlockSpec((tm, d), lambda i, j, p: (i, 0)),
            pl.BlockSpec((None, 1, d), lambda i, j, p: (layer, 0, 0)),
            pl.BlockSpec((None, d, tn), lambda i, j, p: (layer, 0, j)),
            pl.BlockSpec((None, d, 2 * KV_WIDTH), lambda i, j, p: (layer, 0, 0)),
            pl.BlockSpec((tm, HEAD_DIM), lambda i, j, p: (p[i], 0)),
            pl.BlockSpec((tm, HEAD_DIM), lambda i, j, p: (p[i], 0)),
        ],
        out_specs=[
            pl.BlockSpec((tm, tn), lambda i, j, p: (i, j)),
            pl.BlockSpec((tm, 2 * KV_WIDTH), lambda i, j, p: (i, 0)),
        ],
        scratch_shapes=[pltpu.VMEM((tm, d), BF16)],
    )
    return pl.pallas_call(
        _in_proj_kernel,
        grid_spec=grid_spec,
        out_shape=[jax.ShapeDtypeStruct((t, n_main), BF16),
                   jax.ShapeDtypeStruct((t, 2 * KV_WIDTH), BF16)],
        compiler_params=_params(("arbitrary", "arbitrary")),
        name="in_proj",
    )(pos_blk, x, g, w_main, w_kv, cos_t, sin_t)


def _attn_kernel(edge_ref, sink_ref, q_ref, k_ref, v_ref, kp_ref, vp_ref, kn_ref, vn_ref,
                 o_ref, kbuf, vbuf):
    i = pl.program_id(0)
    rows = q_ref.shape[0]
    nblk = rows // BLOCK
    has_prev = edge_ref[0, i]
    has_next = edge_ref[1, i]
    kbuf[0:BLOCK, :] = kp_ref[...]
    kbuf[BLOCK:BLOCK + rows, :] = k_ref[...]
    kbuf[BLOCK + rows:, :] = kn_ref[...]
    vbuf[0:BLOCK, :] = vp_ref[...]
    vbuf[BLOCK:BLOCK + rows, :] = v_ref[...]
    vbuf[BLOCK + rows:, :] = vn_ref[...]

    r_io = lax.broadcasted_iota(jnp.int32, (BLOCK, 3 * BLOCK), 0)
    s_io = lax.broadcasted_iota(jnp.int32, (BLOCK, 3 * BLOCK), 1)
    scale = HEAD_DIM ** -0.5

    def block_body(b, carry):
        lo = jnp.where(jnp.logical_or(b > 0, has_prev == 1), 0, BLOCK)
        hi = jnp.where(jnp.logical_or(b < nblk - 1, has_next == 1), 3 * BLOCK, 2 * BLOCK)
        valid = (s_io >= jnp.maximum(r_io, lo)) & (s_io <= r_io + 2 * BLOCK) & (s_io < hi)
        q0 = pl.multiple_of(b * BLOCK, BLOCK)
        for c in range(N_KV_HEADS):
            csl = slice(c * HEAD_DIM, (c + 1) * HEAD_DIM)
            kb = kbuf[pl.ds(q0, 3 * BLOCK), csl]
            vb = vbuf[pl.ds(q0, 3 * BLOCK), csl]
            qs = jnp.concatenate(
                [q_ref[pl.ds(q0, BLOCK), (c * GROUP + g) * HEAD_DIM:(c * GROUP + g + 1) * HEAD_DIM]
                 for g in range(GROUP)], axis=0)
            s = lax.dot_general(qs, kb, (((1,), (1,)), ((), ())),
                                preferred_element_type=F32) * scale
            ps, invs = [], []
            for g in range(GROUP):
                sink = sink_ref[c * GROUP + g]
                sg = jnp.where(valid, s[g * BLOCK:(g + 1) * BLOCK], NEG_INF)
                m = jnp.maximum(jnp.max(sg, axis=-1, keepdims=True), sink)
                p = jnp.exp(sg - m)
                den = jnp.sum(p, axis=-1, keepdims=True) + jnp.exp(sink - m)
                ps.append(p.astype(BF16))
                invs.append(1.0 / den)
            o = jnp.dot(jnp.concatenate(ps, axis=0), vb, preferred_element_type=F32)
            for g in range(GROUP):
                hsl = slice((c * GROUP + g) * HEAD_DIM, (c * GROUP + g + 1) * HEAD_DIM)
                o_ref[pl.ds(q0, BLOCK), hsl] = (o[g * BLOCK:(g + 1) * BLOCK] * invs[g]).astype(BF16)
        return carry

    lax.fori_loop(0, nblk, block_body, 0)


def _attn(proj, kv, sink, edges, tiles):
    t = proj.shape[0]
    tq = tiles.attn_rows
    assert t % tq == 0 and tq % BLOCK == 0
    per = tq // BLOCK
    last_blk = t // BLOCK - 1
    prev_map = lambda i, e: (jnp.maximum(i * per - 1, 0), 0)
    next_map = lambda i, e: (jnp.minimum((i + 1) * per, last_blk), 0)
    prev_map_v = lambda i, e: (jnp.maximum(i * per - 1, 0), 1)
    next_map_v = lambda i, e: (jnp.minimum((i + 1) * per, last_blk), 1)
    grid_spec = pltpu.PrefetchScalarGridSpec(
        num_scalar_prefetch=1,
        grid=(t // tq,),
        in_specs=[
            pl.BlockSpec(memory_space=pltpu.SMEM),
            pl.BlockSpec((tq, ATTN_WIDTH), lambda i, e: (i, 0)),
            pl.BlockSpec((tq, KV_WIDTH), lambda i, e: (i, 0)),
            pl.BlockSpec((tq, KV_WIDTH), lambda i, e: (i, 1)),
            pl.BlockSpec((BLOCK, KV_WIDTH), prev_map),
            pl.BlockSpec((BLOCK, KV_WIDTH), prev_map_v),
            pl.BlockSpec((BLOCK, KV_WIDTH), next_map),
            pl.BlockSpec((BLOCK, KV_WIDTH), next_map_v),
        ],
        out_specs=pl.BlockSpec((tq, ATTN_WIDTH), lambda i, e: (i, 0)),
        scratch_shapes=[pltpu.VMEM((tq + 2 * BLOCK, KV_WIDTH), BF16),
                        pltpu.VMEM((tq + 2 * BLOCK, KV_WIDTH), BF16)],
    )
    return pl.pallas_call(
        _attn_kernel,
        grid_spec=grid_spec,
        out_shape=jax.ShapeDtypeStruct((t, ATTN_WIDTH), BF16),
        compiler_params=_params(("arbitrary",)),
        name="attn",
    )(edges, sink, proj, kv, kv, kv, kv, kv, kv)


def _mix_out_kernel(edge_ref, attn_ref, b_ref, c_ref, xc_ref, cp_ref, xp_ref, cn_ref, xn_ref,
                    ga_ref, gc_ref, bgate_ref, wconv_ref, wao_ref, wco_ref, wout_ref,
                    x_ref, gpost_ref, o_ref):
    i = pl.program_id(0)
    rows, d = x_ref.shape

    u = c_ref[...].astype(F32) * xc_ref[...].astype(F32)
    up = (cp_ref[SUBLANES - 1:SUBLANES, :].astype(F32)
          * xp_ref[SUBLANES - 1:SUBLANES, :].astype(F32))
    un = cn_ref[0:1, :].astype(F32) * xn_ref[0:1, :].astype(F32)
    up = jnp.where(edge_ref[0, i] == 1, up, 0.0)
    un = jnp.where(edge_ref[1, i] == 1, un, 0.0)
    r_io = lax.broadcasted_iota(jnp.int32, u.shape, 0)
    u_prev = jnp.where(r_io == 0, up, pltpu.roll(u, 1, 0))
    u_next = jnp.where(r_io == rows - 1, un, pltpu.roll(u, rows - 1, 0))
    conv = u_prev * wconv_ref[0:1, :] + u * wconv_ref[1:2, :] + u_next * wconv_ref[2:3, :]
    conv = (b_ref[...].astype(F32) * conv).astype(BF16)

    o_a = jnp.dot(attn_ref[...], wao_ref[...], preferred_element_type=F32)
    o_c = jnp.dot(conv, wco_ref[...], preferred_element_type=F32)
    merged = (jax.nn.sigmoid(ga_ref[...].astype(F32) + bgate_ref[:, :d]) * o_a
              + jax.nn.sigmoid(gc_ref[...].astype(F32) + bgate_ref[:, d:]) * o_c)
    y = jnp.dot(merged.astype(BF16), wout_ref[...], preferred_element_type=F32)
    o_ref[...] = x_ref[...] + _rmsnorm(y, gpost_ref[...])


def _mix_out(layer, x, attn, proj, b_gate, w_conv, w_ao, w_co, w_out, g_post, edges, tiles):
    t, d = x.shape
    cw = w_co.shape[1]
    tm = tiles.mix_rows
    assert t % tm == 0 and tm % SUBLANES == 0
    assert ATTN_WIDTH == cw and (ATTN_WIDTH + 3 * cw) % d == 0
    ga_blk = (ATTN_WIDTH + 3 * cw) // d
    per8 = tm // SUBLANES
    last8 = t // SUBLANES - 1
    prev8 = lambda col: (lambda i, e: (jnp.maximum(i * per8 - 1, 0), col))
    next8 = lambda col: (lambda i, e: (jnp.minimum((i + 1) * per8, last8), col))
    resident = pl.Buffered(1)
    grid_spec = pltpu.PrefetchScalarGridSpec(
        num_scalar_prefetch=1,
        grid=(t // tm,),
        in_specs=[
            pl.BlockSpec((tm, ATTN_WIDTH), lambda i, e: (i, 0)),
            pl.BlockSpec((tm, cw), lambda i, e: (i, 1)),
            pl.BlockSpec((tm, cw), lambda i, e: (i, 2)),
            pl.BlockSpec((tm, cw), lambda i, e: (i, 3)),
            pl.BlockSpec((SUBLANES, cw), prev8(2)),
            pl.BlockSpec((SUBLANES, cw), prev8(3)),
            pl.BlockSpec((SUBLANES, cw), next8(2)),
            pl.BlockSpec((SUBLANES, cw), next8(3)),
            pl.BlockSpec((tm, d), lambda i, e: (i, ga_blk)),
            pl.BlockSpec((tm, d), lambda i, e: (i, ga_blk + 1)),
            pl.BlockSpec((None, 1, 2 * d), lambda i, e: (layer, 0, 0)),
            pl.BlockSpec((None, 3, cw), lambda i, e: (layer, 0, 0)),
            pl.BlockSpec((None, ATTN_WIDTH, d), lambda i, e: (layer, 0, 0), pipeline_mode=resident),
            pl.BlockSpec((None, cw, d), lambda i, e: (layer, 0, 0), pipeline_mode=resident),
            pl.BlockSpec((None, d, d), lambda i, e: (layer, 0, 0), pipeline_mode=resident),
            pl.BlockSpec((tm, d), lambda i, e: (i, 0)),
            pl.BlockSpec((None, 1, d), lambda i, e: (layer, 0, 0)),
        ],
        out_specs=pl.BlockSpec((tm, d), lambda i, e: (i, 0)),
    )
    return pl.pallas_call(
        _mix_out_kernel,
        grid_spec=grid_spec,
        out_shape=jax.ShapeDtypeStruct((t, d), F32),
        compiler_params=_params(("arbitrary",)),
        name="mix_out",
    )(edges, attn, proj, proj, proj, proj, proj, proj, proj, proj, proj,
      b_gate, w_conv, w_ao, w_co, w_out, x, g_post)


def _mlp_kernel(x_ref, gpre_ref, w1_ref, w2_ref, gpost_ref, o_ref, h_ref):
    j = pl.program_id(1)

    @pl.when(j == 0)
    def _():
        h_ref[...] = _rmsnorm(x_ref[...], gpre_ref[...]).astype(BF16)
        o_ref[...] = jnp.zeros_like(o_ref)

    a = jnp.dot(h_ref[...], w1_ref[...], preferred_element_type=F32)
    a = jnp.square(jnp.maximum(a, 0.0)).astype(BF16)
    o_ref[...] += jnp.dot(a, w2_ref[...], preferred_element_type=F32)

    @pl.when(j == pl.num_programs(1) - 1)
    def _():
        o_ref[...] = x_ref[...] + _rmsnorm(o_ref[...], gpost_ref[...])


def _mlp(layer, x, g_pre, w1, w2, g_post, tiles):
    t, d = x.shape
    f = w1.shape[2]
    tm, tf = tiles.mlp_rows, tiles.mlp_cols
    assert t % tm == 0 and f % tf == 0
    return pl.pallas_call(
        _mlp_kernel,
        grid=(t // tm, f // tf),
        in_specs=[
            pl.BlockSpec((tm, d), lambda i, j: (i, 0)),
            pl.BlockSpec((None, 1, d), lambda i, j: (layer, 0, 0)),
            pl.BlockSpec((None, d, tf), lambda i, j: (layer, 0, j)),
            pl.BlockSpec((None, tf, d), lambda i, j: (layer, j, 0)),
            pl.BlockSpec((None, 1, d), lambda i, j: (layer, 0, 0)),
        ],
        out_specs=pl.BlockSpec((tm, d), lambda i, j: (i, 0)),
        out_shape=jax.ShapeDtypeStruct((t, d), F32),
        scratch_shapes=[pltpu.VMEM((tm, d), BF16)],
        compiler_params=_params(("arbitrary", "arbitrary")),
        name="mlp",
    )(x, g_pre, w1, w2, g_post)


def _tile_edges(seq_lens, tile):
    prev, nxt = [], []
    for s in seq_lens:
        assert s % tile == 0
        n = s // tile
        prev += [0] + [1] * (n - 1)
        nxt += [1] * (n - 1) + [0]
    return jnp.asarray(np.array([prev, nxt], dtype=np.int32))


def _tile_pos_blocks(seq_lens, tile):
    out = []
    for s in seq_lens:
        assert s % tile == 0
        out += list(range(s // tile))
    return jnp.asarray(np.array(out, dtype=np.int32))


def _rope_tables(max_len):
    inv_freq = ROPE_THETA ** (-jnp.arange(0, ROT_DIM, 2, dtype=F32) / ROT_DIM)
    ang = jnp.arange(max_len).astype(F32)[:, None] * inv_freq[None, :]
    cos, sin = jnp.cos(ang), jnp.sin(ang)
    rest = HEAD_DIM - ROT_DIM
    cos_t = jnp.concatenate([cos, cos, jnp.ones((max_len, rest), F32)], axis=-1)
    sin_t = jnp.concatenate([-sin, sin, jnp.zeros((max_len, rest), F32)], axis=-1)
    return cos_t, sin_t


def _trunk(groups, params, tiles):
    (g_pre_mix, w_in, b_gate, w_sink, w_conv, w_attn_out, w_conv_out, w_out,
     g_post_mix, g_pre_mlp, w_mlp_in, w_mlp_out, g_post_mlp) = params
    depth, d, _ = w_in.shape
    seq_lens = [g.shape[1] for g in groups for _ in range(g.shape[0])]
    x = jnp.concatenate([g.reshape(-1, d) for g in groups], axis=0)

    cos_t, sin_t = _rope_tables(max(seq_lens))
    pos_blk = _tile_pos_blocks(seq_lens, tiles.proj_rows)
    attn_edges = _tile_edges(seq_lens, tiles.attn_rows)
    mix_edges = _tile_edges(seq_lens, tiles.mix_rows)

    kv0 = ATTN_WIDTH
    rest0 = ATTN_WIDTH + 2 * KV_WIDTH
    w_main = jnp.concatenate([w_in[:, :, :kv0], w_in[:, :, rest0:]], axis=2).astype(BF16)
    w_kv = w_in[:, :, kv0:rest0].astype(BF16)
    w_ao, w_co, w_o = (w.astype(BF16) for w in (w_attn_out, w_conv_out, w_out))
    w1, w2 = w_mlp_in.astype(BF16), w_mlp_out.astype(BF16)
    row = lambda p: p[:, None, :]

    for l in range(depth):
        proj, kv = _in_proj(l, x, row(g_pre_mix), w_main, w_kv, cos_t, sin_t, pos_blk, tiles)
        attn = _attn(proj, kv, w_sink[l], attn_edges, tiles)
        x = _mix_out(l, x, attn, proj, row(b_gate), w_conv, w_ao, w_co, w_o, row(g_post_mix),
                     mix_edges, tiles)
        x = _mlp(l, x, row(g_pre_mlp), w1, w2, row(g_post_mlp), tiles)

    outs, r0 = [], 0
    for g in groups:
        n = g.shape[0] * g.shape[1]
        outs.append(x[r0:r0 + n].reshape(g.shape))
        r0 += n
    return tuple(outs)


def kernel(x_prompt, x_sample, g_pre_mix, w_in, b_gate, w_sink, w_conv, w_attn_out, w_conv_out,
           w_out, g_post_mix, g_pre_mlp, w_mlp_in, w_mlp_out, g_post_mlp):
    params = (g_pre_mix, w_in, b_gate, w_sink, w_conv, w_attn_out, w_conv_out, w_out,
              g_post_mix, g_pre_mlp, w_mlp_in, w_mlp_out, g_post_mlp)
    return _trunk([x_prompt, x_sample], params, V7X_TILES)
```

```python
from typing import NamedTuple

import numpy as np
import jax
import jax.numpy as jnp
from jax import lax
from jax.experimental import pallas as pl
from jax.experimental.pallas import tpu as pltpu

F32 = jnp.float32
BF16 = jnp.bfloat16

HEAD_DIM = 128
N_Q_HEADS = 8
N_KV_HEADS = 2
GROUP = N_Q_HEADS // N_KV_HEADS
ATTN_WIDTH = N_Q_HEADS * HEAD_DIM
KV_WIDTH = N_KV_HEADS * HEAD_DIM
BLOCK = 128
ROT_DIM = HEAD_DIM // 4
ROT_HALF = ROT_DIM // 2
ROPE_THETA = 500000.0
RMS_EPS = 1e-6
NEG_INF = -1e30
LOG2_E = 1.4426950408889634
SUBLANES = 8
V7X_VMEM_LIMIT_BYTES = 60000 * 1024


class Tiles(NamedTuple):
    proj_rows: int
    proj_cols: int
    attn_rows: int
    mix_rows: int
    mlp_rows: int
    mlp_cols: int


V7X_TILES = Tiles(proj_rows=768, proj_cols=2048, attn_rows=1024, mix_rows=512,
                  mlp_rows=768, mlp_cols=1024)


def _rmsnorm(x, g):
    return x * lax.rsqrt(jnp.mean(x * x, axis=-1, keepdims=True) + RMS_EPS) * g


def _rope_head(xh, cos, sin, lane):
    partner = jnp.where(lane < ROT_HALF,
                        pltpu.roll(xh, HEAD_DIM - ROT_HALF, 1),
                        pltpu.roll(xh, ROT_HALF, 1))
    return jnp.where(lane < ROT_DIM, xh * cos + partner * sin, xh)


def _params(semantics):
    return pltpu.CompilerParams(dimension_semantics=semantics,
                                vmem_limit_bytes=V7X_VMEM_LIMIT_BYTES)


def _in_proj_kernel(x_ref, g_ref, w_ref, wkv_ref, cos_ref, sin_ref, proj_ref, kv_ref, h_ref):
    j = pl.program_id(1)
    rows = x_ref.shape[0]
    lane = lax.broadcasted_iota(jnp.int32, (rows, HEAD_DIM), 1)

    @pl.when(j == 0)
    def _():
        h_ref[...] = _rmsnorm(x_ref[...], g_ref[...]).astype(BF16)
        cos = cos_ref[...]
        sin = sin_ref[...]
        kv = jnp.dot(h_ref[...], wkv_ref[...], preferred_element_type=F32)
        for hd in range(N_KV_HEADS):
            sl = slice(hd * HEAD_DIM, (hd + 1) * HEAD_DIM)
            kv_ref[:, sl] = _rope_head(kv[:, sl], cos, sin, lane).astype(BF16)
        kv_ref[:, KV_WIDTH:] = kv[:, KV_WIDTH:].astype(BF16)
        q = jnp.dot(h_ref[...], w_ref[...], preferred_element_type=F32)
        for hd in range(N_Q_HEADS):
            sl = slice(hd * HEAD_DIM, (hd + 1) * HEAD_DIM)
            proj_ref[:, sl] = _rope_head(q[:, sl], cos, sin, lane).astype(BF16)
        if q.shape[1] > ATTN_WIDTH:
            proj_ref[:, ATTN_WIDTH:] = q[:, ATTN_WIDTH:].astype(BF16)

    @pl.when(j != 0)
    def _():
        proj_ref[...] = jnp.dot(h_ref[...], w_ref[...],
                                preferred_element_type=F32).astype(BF16)


def _in_proj(layer, x, g, w_main, w_kv, cos_rows, sin_rows, tiles):
    t, d = x.shape
    n_main = w_main.shape[2]
    tm, tn = tiles.proj_rows, tiles.proj_cols
    assert tn % ATTN_WIDTH == 0 and n_main % tn == 0 and t % tm == 0
    return pl.pallas_call(
        _in_proj_kernel,
        grid=(t // tm, n_main // tn),
        in_specs=[
            pl.BlockSpec((tm, d), lambda i, j: (i, 0)),
            pl.BlockSpec((None, 1, d), lambda i, j: (layer, 0, 0)),
            pl.BlockSpec((None, d, tn), lambda i, j: (layer, 0, j)),
            pl.BlockSpec((None, d, 2 * KV_WIDTH), lambda i, j: (layer, 0, 0),
                         pipeline_mode=pl.Buffered(1)),
            pl.BlockSpec((tm, HEAD_DIM), lambda i, j: (i, 0)),
            pl.BlockSpec((tm, HEAD_DIM), lambda i, j: (i, 0)),
        ],
        out_specs=[
            pl.BlockSpec((tm, tn), lambda i, j: (i, j)),
            pl.BlockSpec((tm, 2 * KV_WIDTH), lambda i, j: (i, 0)),
        ],
        scratch_shapes=[pltpu.VMEM((tm, d), BF16)],
        out_shape=[jax.ShapeDtypeStruct((t, n_main), BF16),
                   jax.ShapeDtypeStruct((t, 2 * KV_WIDTH), BF16)],
        compiler_params=_params(("arbitrary", "arbitrary")),
        name="in_proj",
    )(x, g, w_main, w_kv, cos_rows, sin_rows)


def _attn_kernel(edge_ref, sink_ref, q_ref, k_ref, v_ref, kp_ref, vp_ref, kn_ref, vn_ref,
                 o_ref, kbuf, vbuf):
    i = pl.program_id(0)
    rows = q_ref.shape[0]
    nblk = rows // BLOCK
    has_prev = edge_ref[0, i]
    has_next = edge_ref[1, i]
    kbuf[0:BLOCK, :] = kp_ref[...]
    kbuf[BLOCK:BLOCK + rows, :] = k_ref[...]
    kbuf[BLOCK + rows:, :] = kn_ref[...]
    vbuf[0:BLOCK, :] = vp_ref[...]
    vbuf[BLOCK:BLOCK + rows, :] = v_ref[...]
    vbuf[BLOCK + rows:, :] = vn_ref[...]

    r_io = lax.broadcasted_iota(jnp.int32, (BLOCK, BLOCK), 0)
    s_io = lax.broadcasted_iota(jnp.int32, (BLOCK, BLOCK), 1)
    scale = HEAD_DIM ** -0.5 * LOG2_E

    def block_body(b, carry):
        valid_prev = (s_io >= r_io) & jnp.logical_or(b > 0, has_prev == 1)
        valid_next = (s_io <= r_io) & jnp.logical_or(b < nblk - 1, has_next == 1)
        q0 = pl.multiple_of(b * BLOCK, BLOCK)
        for c in range(N_KV_HEADS):
            csl = slice(c * HEAD_DIM, (c + 1) * HEAD_DIM)
            kb = kbuf[pl.ds(q0, 3 * BLOCK), csl]
            vb = vbuf[pl.ds(q0, 3 * BLOCK), csl]
            qs = jnp.concatenate(
                [q_ref[pl.ds(q0, BLOCK), (c * GROUP + g) * HEAD_DIM:(c * GROUP + g + 1) * HEAD_DIM]
                 for g in range(GROUP)], axis=0)
            s = lax.dot_general(qs, kb, (((1,), (1,)), ((), ())),
                                preferred_element_type=F32) * scale
            ps, invs = [], []
            for g in range(GROUP):
                sink = sink_ref[c * GROUP + g] * LOG2_E
                sg = s[g * BLOCK:(g + 1) * BLOCK]
                sg = jnp.concatenate(
                    [jnp.where(valid_prev, sg[:, :BLOCK], NEG_INF),
                     sg[:, BLOCK:2 * BLOCK],
                     jnp.where(valid_next, sg[:, 2 * BLOCK:], NEG_INF)], axis=1)
                m = jnp.maximum(jnp.max(sg, axis=-1, keepdims=True), sink)
                p = jnp.exp2(sg - m)
                den = jnp.sum(p, axis=-1, keepdims=True) + jnp.exp2(sink - m)
                ps.append(p.astype(BF16))
                invs.append(1.0 / den)
            o = jnp.dot(jnp.concatenate(ps, axis=0), vb, preferred_element_type=F32)
            for g in range(GROUP):
                hsl = slice((c * GROUP + g) * HEAD_DIM, (c * GROUP + g + 1) * HEAD_DIM)
                o_ref[pl.ds(q0, BLOCK), hsl] = (o[g * BLOCK:(g + 1) * BLOCK] * invs[g]).astype(BF16)
        return carry

    lax.fori_loop(0, nblk, block_body, 0)


def _attn(proj, kv, sink, edges, tiles):
    t = proj.shape[0]
    tq = tiles.attn_rows
    assert t % tq == 0 and tq % BLOCK == 0
    per = tq // BLOCK
    last_blk = t // BLOCK - 1
    prev_map = lambda i, e: (jnp.maximum(i * per - 1, 0), 0)
    next_map = lambda i, e: (jnp.minimum((i + 1) * per, last_blk), 0)
    prev_map_v = lambda i, e: (jnp.maximum(i * per - 1, 0), 1)
    next_map_v = lambda i, e: (jnp.minimum((i + 1) * per, last_blk), 1)
    grid_spec = pltpu.PrefetchScalarGridSpec(
        num_scalar_prefetch=1,
        grid=(t // tq,),
        in_specs=[
            pl.BlockSpec(memory_space=pltpu.SMEM),
            pl.BlockSpec((tq, ATTN_WIDTH), lambda i, e: (i, 0)),
            pl.BlockSpec((tq, KV_WIDTH), lambda i, e: (i, 0)),
            pl.BlockSpec((tq, KV_WIDTH), lambda i, e: (i, 1)),
            pl.BlockSpec((BLOCK, KV_WIDTH), prev_map),
            pl.BlockSpec((BLOCK, KV_WIDTH), prev_map_v),
            pl.BlockSpec((BLOCK, KV_WIDTH), next_map),
            pl.BlockSpec((BLOCK, KV_WIDTH), next_map_v),
        ],
        out_specs=pl.BlockSpec((tq, ATTN_WIDTH), lambda i, e: (i, 0)),
        scratch_shapes=[pltpu.VMEM((tq + 2 * BLOCK, KV_WIDTH), BF16),
                        pltpu.VMEM((tq + 2 * BLOCK, KV_WIDTH), BF16)],
    )
    return pl.pallas_call(
        _attn_kernel,
        grid_spec=grid_spec,
        out_shape=jax.ShapeDtypeStruct((t, ATTN_WIDTH), BF16),
        compiler_params=_params(("arbitrary",)),
        name="attn",
    )(edges, sink, proj, kv, kv, kv, kv, kv, kv)


def _mix_out_kernel(edge_ref, attn_ref, b_ref, c_ref, xc_ref, cp_ref, xp_ref, cn_ref, xn_ref,
                    ga_ref, gc_ref, bgate_ref, wconv_ref, wao_ref, wco_ref, wout_ref,
                    x_ref, gpost_ref, o_ref):
    i = pl.program_id(0)
    rows, d = x_ref.shape

    u = c_ref[...].astype(F32) * xc_ref[...].astype(F32)
    up = (cp_ref[SUBLANES - 1:SUBLANES, :].astype(F32)
          * xp_ref[SUBLANES - 1:SUBLANES, :].astype(F32))
    un = cn_ref[0:1, :].astype(F32) * xn_ref[0:1, :].astype(F32)
    up = jnp.where(edge_ref[0, i] == 1, up, 0.0)
    un = jnp.where(edge_ref[1, i] == 1, un, 0.0)
    r_io = lax.broadcasted_iota(jnp.int32, u.shape, 0)
    u_prev = jnp.where(r_io == 0, up, pltpu.roll(u, 1, 0))
    u_next = jnp.where(r_io == rows - 1, un, pltpu.roll(u, rows - 1, 0))
    conv = u_prev * wconv_ref[0:1, :] + u * wconv_ref[1:2, :] + u_next * wconv_ref[2:3, :]
    conv = (b_ref[...].astype(F32) * conv).astype(BF16)

    o_a = jnp.dot(attn_ref[...], wao_ref[...], preferred_element_type=F32)
    o_c = jnp.dot(conv, wco_ref[...], preferred_element_type=F32)
    merged = (jax.nn.sigmoid(ga_ref[...].astype(F32) + bgate_ref[:, :d]) * o_a
              + jax.nn.sigmoid(gc_ref[...].astype(F32) + bgate_ref[:, d:]) * o_c)
    y = jnp.dot(merged.astype(BF16), wout_ref[...], preferred_element_type=F32)
    o_ref[...] = x_ref[...] + _rmsnorm(y, gpost_ref[...])


def _mix_out(layer, x, attn, proj, b_gate, w_conv, w_ao, w_co, w_out, g_post, edges, tiles):
    t, d = x.shape
    cw = w_co.shape[1]
    tm = tiles.mix_rows
    assert t % tm == 0 and tm % SUBLANES == 0
    assert ATTN_WIDTH == cw and (ATTN_WIDTH + 3 * cw) % d == 0
    ga_blk = (ATTN_WIDTH + 3 * cw) // d
    per8 = tm // SUBLANES
    last8 = t // SUBLANES - 1
    prev8 = lambda col: (lambda i, e: (jnp.maximum(i * per8 - 1, 0), col))
    next8 = lambda col: (lambda i, e: (jnp.minimum((i + 1) * per8, last8), col))
    resident = pl.Buffered(1)
    grid_spec = pltpu.PrefetchScalarGridSpec(
        num_scalar_prefetch=1,
        grid=(t // tm,),
        in_specs=[
            pl.BlockSpec((tm, ATTN_WIDTH), lambda i, e: (i, 0)),
            pl.BlockSpec((tm, cw), lambda i, e: (i, 1)),
            pl.BlockSpec((tm, cw), lambda i, e: (i, 2)),
            pl.BlockSpec((tm, cw), lambda i, e: (i, 3)),
            pl.BlockSpec((SUBLANES, cw), prev8(2)),
            pl.BlockSpec((SUBLANES, cw), prev8(3)),
            pl.BlockSpec((SUBLANES, cw), next8(2)),
            pl.BlockSpec((SUBLANES, cw), next8(3)),
            pl.BlockSpec((tm, d), lambda i, e: (i, ga_blk)),
            pl.BlockSpec((tm, d), lambda i, e: (i, ga_blk + 1)),
            pl.BlockSpec((None, 1, 2 * d), lambda i, e: (layer, 0, 0)),
            pl.BlockSpec((None, 3, cw), lambda i, e: (layer, 0, 0)),
            pl.BlockSpec((None, ATTN_WIDTH, d), lambda i, e: (layer, 0, 0), pipeline_mode=resident),
            pl.BlockSpec((None, cw, d), lambda i, e: (layer, 0, 0), pipeline_mode=resident),
            pl.BlockSpec((None, d, d), lambda i, e: (layer, 0, 0), pipeline_mode=resident),
            pl.BlockSpec((tm, d), lambda i, e: (i, 0)),
            pl.BlockSpec((None, 1, d), lambda i, e: (layer, 0, 0)),
        ],
        out_specs=pl.BlockSpec((tm, d), lambda i, e: (i, 0)),
    )
    return pl.pallas_call(
        _mix_out_kernel,
        grid_spec=grid_spec,
        out_shape=jax.ShapeDtypeStruct((t, d), F32),
        compiler_params=_params(("arbitrary",)),
        name="mix_out",
    )(edges, attn, proj, proj, proj, proj, proj, proj, proj, proj, proj,
      b_gate, w_conv, w_ao, w_co, w_out, x, g_post)


def _mlp_kernel(x_ref, gpre_ref, w1_ref, w2_ref, gpost_ref, o_ref, h_ref):
    j = pl.program_id(1)

    @pl.when(j == 0)
    def _():
        h_ref[...] = _rmsnorm(x_ref[...], gpre_ref[...]).astype(BF16)
        o_ref[...] = jnp.zeros_like(o_ref)

    a = jnp.dot(h_ref[...], w1_ref[...], preferred_element_type=F32)
    a = jnp.square(jnp.maximum(a, 0.0)).astype(BF16)
    o_ref[...] += jnp.dot(a, w2_ref[...], preferred_element_type=F32)

    @pl.when(j == pl.num_programs(1) - 1)
    def _():
        o_ref[...] = x_ref[...] + _rmsnorm(o_ref[...], gpost_ref[...])


def _mlp(layer, x, g_pre, w1, w2, g_post, tiles):
    t, d = x.shape
    f = w1.shape[2]
    tm, tf = tiles.mlp_rows, tiles.mlp_cols
    assert t % tm == 0 and f % tf == 0
    return pl.pallas_call(
        _mlp_kernel,
        grid=(t // tm, f // tf),
        in_specs=[
            pl.BlockSpec((tm, d), lambda i, j: (i, 0)),
            pl.BlockSpec((None, 1, d), lambda i, j: (layer, 0, 0)),
            pl.BlockSpec((None, d, tf), lambda i, j: (layer, 0, j)),
            pl.BlockSpec((None, tf, d), lambda i, j: (layer, j, 0)),
            pl.BlockSpec((None, 1, d), lambda i, j: (layer, 0, 0)),
        ],
        out_specs=pl.BlockSpec((tm, d), lambda i, j: (i, 0)),
        out_shape=jax.ShapeDtypeStruct((t, d), F32),
        scratch_shapes=[pltpu.VMEM((tm, d), BF16)],
        compiler_params=_params(("arbitrary", "arbitrary")),
        name="mlp",
    )(x, g_pre, w1, w2, g_post)


def _tile_edges(seq_lens, tile):
    prev, nxt = [], []
    for s in seq_lens:
        assert s % tile == 0
        n = s // tile
        prev += [0] + [1] * (n - 1)
        nxt += [1] * (n - 1) + [0]
    return jnp.asarray(np.array([prev, nxt], dtype=np.int32))


def _rope_tables(seq_lens):
    max_len = max(seq_lens)
    inv_freq = ROPE_THETA ** (-jnp.arange(0, ROT_DIM, 2, dtype=F32) / ROT_DIM)
    ang = jnp.arange(max_len).astype(F32)[:, None] * inv_freq[None, :]
    cos, sin = jnp.cos(ang), jnp.sin(ang)
    rest = HEAD_DIM - ROT_DIM
    cos_t = jnp.concatenate([cos, cos, jnp.ones((max_len, rest), F32)], axis=-1)
    sin_t = jnp.concatenate([-sin, sin, jnp.zeros((max_len, rest), F32)], axis=-1)
    per_row = lambda tab: jnp.concatenate([tab[:n] for n in seq_lens], axis=0)
    return per_row(cos_t), per_row(sin_t)


def _trunk(groups, params, tiles):
    (g_pre_mix, w_in, b_gate, w_sink, w_conv, w_attn_out, w_conv_out, w_out,
     g_post_mix, g_pre_mlp, w_mlp_in, w_mlp_out, g_post_mlp) = params
    depth, d, _ = w_in.shape
    seq_lens = [g.shape[1] for g in groups for _ in range(g.shape[0])]
    x = jnp.concatenate([g.reshape(-1, d) for g in groups], axis=0)

    cos_rows, sin_rows = _rope_tables(seq_lens)
    attn_edges = _tile_edges(seq_lens, tiles.attn_rows)
    mix_edges = _tile_edges(seq_lens, tiles.mix_rows)

    kv0 = ATTN_WIDTH
    rest0 = ATTN_WIDTH + 2 * KV_WIDTH
    w_main = jnp.concatenate([w_in[:, :, :kv0], w_in[:, :, rest0:]], axis=2).astype(BF16)
    w_kv = w_in[:, :, kv0:rest0].astype(BF16)
    w_ao, w_co, w_o = (w.astype(BF16) for w in (w_attn_out, w_conv_out, w_out))
    w1, w2 = w_mlp_in.astype(BF16), w_mlp_out.astype(BF16)
    row = lambda p: p[:, None, :]

    for l in range(depth):
        proj, kv = _in_proj(l, x, row(g_pre_mix), w_main, w_kv, cos_rows, sin_rows, tiles)
        attn = _attn(proj, kv, w_sink[l], attn_edges, tiles)
        x = _mix_out(l, x, attn, proj, row(b_gate), w_conv, w_ao, w_co, w_o, row(g_post_mix),
                     mix_edges, tiles)
        x = _mlp(l, x, row(g_pre_mlp), w1, w2, row(g_post_mlp), tiles)

    outs, r0 = [], 0
    for g in groups:
        n = g.shape[0] * g.shape[1]
        outs.append(x[r0:r0 + n].reshape(g.shape))
        r0 += n
    return tuple(outs)


def kernel(x_prompt, x_sample, g_pre_mix, w_in, b_gate, w_sink, w_conv, w_attn_out, w_conv_out,
           w_out, g_post_mix, g_pre_mlp, w_mlp_in, w_mlp_out, g_post_mlp):
    params = (g_pre_mix, w_in, b_gate, w_sink, w_conv, w_attn_out, w_conv_out, w_out,
              g_post_mix, g_pre_mlp, w_mlp_in, w_mlp_out, g_post_mlp)
    return _trunk([x_prompt, x_sample], params, V7X_TILES)
```

```python
from typing import NamedTuple

import numpy as np
import jax
import jax.numpy as jnp
from jax import lax
from jax.experimental import pallas as pl
from jax.experimental.pallas import tpu as pltpu

F32 = jnp.float32
BF16 = jnp.bfloat16

HEAD_DIM = 128
N_Q_HEADS = 8
N_KV_HEADS = 2
GROUP = N_Q_HEADS // N_KV_HEADS
ATTN_WIDTH = N_Q_HEADS * HEAD_DIM
KV_WIDTH = N_KV_HEADS * HEAD_DIM
BLOCK = 128
ROT_DIM = HEAD_DIM // 4
ROT_HALF = ROT_DIM // 2
ROPE_THETA = 500000.0
RMS_EPS = 1e-6
NEG_INF = -1e30
LOG2_E = 1.4426950408889634
SUBLANES = 8
V7X_VMEM_LIMIT_BYTES = 60000 * 1024


class Tiles(NamedTuple):
    proj_rows: int
    proj_cols: int
    attn_rows: int
    mix_rows: int
    mlp_rows: int
    mlp_cols: int


V7X_TILES = Tiles(proj_rows=768, proj_cols=2048, attn_rows=1024, mix_rows=512,
                  mlp_rows=768, mlp_cols=1024)


def _rmsnorm(x, g):
    return x * lax.rsqrt(jnp.mean(x * x, axis=-1, keepdims=True) + RMS_EPS) * g


def _rope_head(xh, cos, sin, lane):
    partner = jnp.where(lane < ROT_HALF,
                        pltpu.roll(xh, HEAD_DIM - ROT_HALF, 1),
                        pltpu.roll(xh, ROT_HALF, 1))
    return jnp.where(lane < ROT_DIM, xh * cos + partner * sin, xh)


def _params(semantics):
    return pltpu.CompilerParams(dimension_semantics=semantics,
                                vmem_limit_bytes=V7X_VMEM_LIMIT_BYTES)


def _in_proj_kernel(x_ref, g_ref, w_ref, wkv_ref, cos_ref, sin_ref, proj_ref, kv_ref, h_ref):
    j = pl.program_id(1)
    rows = x_ref.shape[0]
    lane = lax.broadcasted_iota(jnp.int32, (rows, HEAD_DIM), 1)

    @pl.when(j == 0)
    def _():
        h_ref[...] = _rmsnorm(x_ref[...], g_ref[...]).astype(BF16)
        cos = cos_ref[...]
        sin = sin_ref[...]
        kv = jnp.dot(h_ref[...], wkv_ref[...], preferred_element_type=F32)
        for hd in range(N_KV_HEADS):
            sl = slice(hd * HEAD_DIM, (hd + 1) * HEAD_DIM)
            kv_ref[:, sl] = _rope_head(kv[:, sl], cos, sin, lane).astype(BF16)
        kv_ref[:, KV_WIDTH:] = kv[:, KV_WIDTH:].astype(BF16)
        q = jnp.dot(h_ref[...], w_ref[...], preferred_element_type=F32)
        for hd in range(N_Q_HEADS):
            sl = slice(hd * HEAD_DIM, (hd + 1) * HEAD_DIM)
            proj_ref[:, sl] = _rope_head(q[:, sl], cos, sin, lane).astype(BF16)
        if q.shape[1] > ATTN_WIDTH:
            proj_ref[:, ATTN_WIDTH:] = q[:, ATTN_WIDTH:].astype(BF16)

    @pl.when(j != 0)
    def _():
        proj_ref[...] = jnp.dot(h_ref[...], w_ref[...],
                                preferred_element_type=F32).astype(BF16)


def _in_proj(layer, x, g, w_main, w_kv, cos_rows, sin_rows, tiles):
    t, d = x.shape
    n_main = w_main.shape[2]
    tm, tn = tiles.proj_rows, tiles.proj_cols
    assert tn % ATTN_WIDTH == 0 and n_main % tn == 0 and t % tm == 0
    return pl.pallas_call(
        _in_proj_kernel,
        grid=(t // tm, n_main // tn),
        in_specs=[
            pl.BlockSpec((tm, d), lambda i, j: (i, 0)),
            pl.BlockSpec((None, 1, d), lambda i, j: (layer, 0, 0)),
            pl.BlockSpec((None, d, tn), lambda i, j: (layer, 0, j)),
            pl.BlockSpec((None, d, 2 * KV_WIDTH), lambda i, j: (layer, 0, 0),
                         pipeline_mode=pl.Buffered(1)),
            pl.BlockSpec((tm, HEAD_DIM), lambda i, j: (i, 0)),
            pl.BlockSpec((tm, HEAD_DIM), lambda i, j: (i, 0)),
        ],
        out_specs=[
            pl.BlockSpec((tm, tn), lambda i, j: (i, j)),
            pl.BlockSpec((tm, 2 * KV_WIDTH), lambda i, j: (i, 0)),
        ],
        scratch_shapes=[pltpu.VMEM((tm, d), BF16)],
        out_shape=[jax.ShapeDtypeStruct((t, n_main), BF16),
                   jax.ShapeDtypeStruct((t, 2 * KV_WIDTH), BF16)],
        compiler_params=_params(("arbitrary", "arbitrary")),
        name="in_proj",
    )(x, g, w_main, w_kv, cos_rows, sin_rows)


def _attn_kernel(edge_ref, sink_ref, q_ref, k_ref, v_ref, kp_ref, vp_ref, kn_ref, vn_ref,
                 o_ref, kbuf, vbuf):
    i = pl.program_id(0)
    rows = q_ref.shape[0]
    nblk = rows // BLOCK
    has_prev = edge_ref[0, i]
    has_next = edge_ref[1, i]
    kbuf[0:BLOCK, :] = kp_ref[...]
    kbuf[BLOCK:BLOCK + rows, :] = k_ref[...]
    kbuf[BLOCK + rows:, :] = kn_ref[...]
    vbuf[0:BLOCK, :] = vp_ref[...]
    vbuf[BLOCK:BLOCK + rows, :] = v_ref[...]
    vbuf[BLOCK + rows:, :] = vn_ref[...]

    r_io = lax.broadcasted_iota(jnp.int32, (BLOCK, BLOCK), 0)
    s_io = lax.broadcasted_iota(jnp.int32, (BLOCK, BLOCK), 1)
    scale = HEAD_DIM ** -0.5 * LOG2_E

    def block_body(b, carry):
        valid_prev = (s_io >= r_io) & jnp.logical_or(b > 0, has_prev == 1)
        valid_next = (s_io <= r_io) & jnp.logical_or(b < nblk - 1, has_next == 1)
        q0 = pl.multiple_of(b * BLOCK, BLOCK)
        for c in range(N_KV_HEADS):
            csl = slice(c * HEAD_DIM, (c + 1) * HEAD_DIM)
            kb = kbuf[pl.ds(q0, 3 * BLOCK), csl]
            vb = vbuf[pl.ds(q0, 3 * BLOCK), csl]
            qs = jnp.concatenate(
                [q_ref[pl.ds(q0, BLOCK), (c * GROUP + g) * HEAD_DIM:(c * GROUP + g + 1) * HEAD_DIM]
                 for g in range(GROUP)], axis=0)
            s = lax.dot_general(qs, kb, (((1,), (1,)), ((), ())),
                                preferred_element_type=F32) * scale
            ps, invs = [], []
            for g in range(GROUP):
                sink = sink_ref[c * GROUP + g] * LOG2_E
                sg = s[g * BLOCK:(g + 1) * BLOCK]
                sg = jnp.concatenate(
                    [jnp.where(valid_prev, sg[:, :BLOCK], NEG_INF),
                     sg[:, BLOCK:2 * BLOCK],
                     jnp.where(valid_next, sg[:, 2 * BLOCK:], NEG_INF)], axis=1)
                m = jnp.maximum(jnp.max(sg, axis=-1, keepdims=True), sink)
                p = jnp.exp2(sg - m)
                den = jnp.sum(p, axis=-1, keepdims=True) + jnp.exp2(sink - m)
                ps.append(p.astype(BF16))
                invs.append(1.0 / den)
            o = jnp.dot(jnp.concatenate(ps, axis=0), vb, preferred_element_type=F32)
            for g in range(GROUP):
                hsl = slice((c * GROUP + g) * HEAD_DIM, (c * GROUP + g + 1) * HEAD_DIM)
                o_ref[pl.ds(q0, BLOCK), hsl] = (o[g * BLOCK:(g + 1) * BLOCK] * invs[g]).astype(BF16)
        return carry

    lax.fori_loop(0, nblk, block_body, 0)


def _attn(proj, kv, sink, edges, tiles):
    t = proj.shape[0]
    tq = tiles.attn_rows
    assert t % tq == 0 and tq % BLOCK == 0
    per = tq // BLOCK
    last_blk = t // BLOCK - 1
    prev_map = lambda i, e: (jnp.maximum(i * per - 1, 0), 0)
    next_map = lambda i, e: (jnp.minimum((i + 1) * per, last_blk), 0)
    prev_map_v = lambda i, e: (jnp.maximum(i * per - 1, 0), 1)
    next_map_v = lambda i, e: (jnp.minimum((i + 1) * per, last_blk), 1)
    grid_spec = pltpu.PrefetchScalarGridSpec(
        num_scalar_prefetch=1,
        grid=(t // tq,),
        in_specs=[
            pl.BlockSpec(memory_space=pltpu.SMEM),
            pl.BlockSpec((tq, ATTN_WIDTH), lambda i, e: (i, 0)),
            pl.BlockSpec((tq, KV_WIDTH), lambda i, e: (i, 0)),
            pl.BlockSpec((tq, KV_WIDTH), lambda i, e: (i, 1)),
            pl.BlockSpec((BLOCK, KV_WIDTH), prev_map),
            pl.BlockSpec((BLOCK, KV_WIDTH), prev_map_v),
            pl.BlockSpec((BLOCK, KV_WIDTH), next_map),
            pl.BlockSpec((BLOCK, KV_WIDTH), next_map_v),
        ],
        out_specs=pl.BlockSpec((tq, ATTN_WIDTH), lambda i, e: (i, 0)),
        scratch_shapes=[pltpu.VMEM((tq + 2 * BLOCK, KV_WIDTH), BF16),
                        pltpu.VMEM((tq + 2 * BLOCK, KV_WIDTH), BF16)],
    )
    return pl.pallas_call(
        _attn_kernel,
        grid_spec=grid_spec,
        out_shape=jax.ShapeDtypeStruct((t, ATTN_WIDTH), BF16),
        compiler_params=_params(("arbitrary",)),
        name="attn",
    )(edges, sink, proj, kv, kv, kv, kv, kv, kv)


def _mix_out_kernel(edge_ref, attn_ref, b_ref, c_ref, xc_ref, cp_ref, xp_ref, cn_ref, xn_ref,
                    ga_ref, gc_ref, bgate_ref, wconv_ref, wao_ref, wco_ref, wout_ref,
                    x_ref, gpost_ref, o_ref):
    i = pl.program_id(0)
    rows, d = x_ref.shape

    u = c_ref[...].astype(F32) * xc_ref[...].astype(F32)
    up = (cp_ref[SUBLANES - 1:SUBLANES, :].astype(F32)
          * xp_ref[SUBLANES - 1:SUBLANES, :].astype(F32))
    un = cn_ref[0:1, :].astype(F32) * xn_ref[0:1, :].astype(F32)
    up = jnp.where(edge_ref[0, i] == 1, up, 0.0)
    un = jnp.where(edge_ref[1, i] == 1, un, 0.0)
    r_io = lax.broadcasted_iota(jnp.int32, u.shape, 0)
    u_prev = jnp.where(r_io == 0, up, pltpu.roll(u, 1, 0))
    u_next = jnp.where(r_io == rows - 1, un, pltpu.roll(u, rows - 1, 0))
    conv = u_prev * wconv_ref[0:1, :] + u * wconv_ref[1:2, :] + u_next * wconv_ref[2:3, :]
    conv = (b_ref[...].astype(F32) * conv).astype(BF16)

    o_a = jnp.dot(attn_ref[...], wao_ref[...], preferred_element_type=F32)
    o_c = jnp.dot(conv, wco_ref[...], preferred_element_type=F32)
    merged = (jax.nn.sigmoid(ga_ref[...].astype(F32) + bgate_ref[:, :d]) * o_a
              + jax.nn.sigmoid(gc_ref[...].astype(F32) + bgate_ref[:, d:]) * o_c)
    y = jnp.dot(merged.astype(BF16), wout_ref[...], preferred_element_type=F32)
    o_ref[...] = x_ref[...] + _rmsnorm(y, gpost_ref[...])


def _mix_out(layer, x, attn, proj, b_gate, w_conv, w_ao, w_co, w_out, g_post, edges, tiles):
    t, d = x.shape
    cw = w_co.shape[1]
    tm = tiles.mix_rows
    assert t % tm == 0 and tm % SUBLANES == 0
    assert ATTN_WIDTH == cw and (ATTN_WIDTH + 3 * cw) % d == 0
    ga_blk = (ATTN_WIDTH + 3 * cw) // d
    per8 = tm // SUBLANES
    last8 = t // SUBLANES - 1
    prev8 = lambda col: (lambda i, e: (jnp.maximum(i * per8 - 1, 0), col))
    next8 = lambda col: (lambda i, e: (jnp.minimum((i + 1) * per8, last8), col))
    resident = pl.Buffered(1)
    grid_spec = pltpu.PrefetchScalarGridSpec(
        num_scalar_prefetch=1,
        grid=(t // tm,),
        in_specs=[
            pl.BlockSpec((tm, ATTN_WIDTH), lambda i, e: (i, 0)),
            pl.BlockSpec((tm, cw), lambda i, e: (i, 1)),
            pl.BlockSpec((tm, cw), lambda i, e: (i, 2)),
            pl.BlockSpec((tm, cw), lambda i, e: (i, 3)),
            pl.BlockSpec((SUBLANES, cw), prev8(2)),
            pl.BlockSpec((SUBLANES, cw), prev8(3)),
            pl.BlockSpec((SUBLANES, cw), next8(2)),
            pl.BlockSpec((SUBLANES, cw), next8(3)),
            pl.BlockSpec((tm, d), lambda i, e: (i, ga_blk)),
            pl.BlockSpec((tm, d), lambda i, e: (i, ga_blk + 1)),
            pl.BlockSpec((None, 1, 2 * d), lambda i, e: (layer, 0, 0)),
            pl.BlockSpec((None, 3, cw), lambda i, e: (layer, 0, 0)),
            pl.BlockSpec((None, ATTN_WIDTH, d), lambda i, e: (layer, 0, 0), pipeline_mode=resident),
            pl.BlockSpec((None, cw, d), lambda i, e: (layer, 0, 0), pipeline_mode=resident),
            pl.BlockSpec((None, d, d), lambda i, e: (layer, 0, 0), pipeline_mode=resident),
            pl.BlockSpec((tm, d), lambda i, e: (i, 0)),
            pl.BlockSpec((None, 1, d), lambda i, e: (layer, 0, 0)),
        ],
        out_specs=pl.BlockSpec((tm, d), lambda i, e: (i, 0)),
    )
    return pl.pallas_call(
        _mix_out_kernel,
        grid_spec=grid_spec,
        out_shape=jax.ShapeDtypeStruct((t, d), F32),
        compiler_params=_params(("arbitrary",)),
        name="mix_out",
    )(edges, attn, proj, proj, proj, proj, proj, proj, proj, proj, proj,
      b_gate, w_conv, w_ao, w_co, w_out, x, g_post)


def _mlp_kernel(x_ref, gpre_ref, w1_ref, w2_ref, gpost_ref, o_ref, h_ref):
    j = pl.program_id(1)
    last = pl.num_programs(1) - 1

    def hidden_chunk():
        a = jnp.dot(h_ref[...], w1_ref[...], preferred_element_type=F32)
        a = jnp.square(jnp.maximum(a, 0.0)).astype(BF16)
        return jnp.dot(a, w2_ref[...], preferred_element_type=F32)

    @pl.when(j == 0)
    def _():
        h_ref[...] = _rmsnorm(x_ref[...], gpre_ref[...]).astype(BF16)
        o_ref[...] = hidden_chunk()

    @pl.when(jnp.logical_and(j > 0, j < last))
    def _():
        o_ref[...] += hidden_chunk()

    @pl.when(j == last)
    def _():
        f = o_ref[...] + hidden_chunk()
        o_ref[...] = x_ref[...] + _rmsnorm(f, gpost_ref[...])


def _mlp(layer, x, g_pre, w1, w2, g_post, tiles):
    t, d = x.shape
    f = w1.shape[2]
    tm, tf = tiles.mlp_rows, tiles.mlp_cols
    assert t % tm == 0 and f % tf == 0
    return pl.pallas_call(
        _mlp_kernel,
        grid=(t // tm, f // tf),
        in_specs=[
            pl.BlockSpec((tm, d), lambda i, j: (i, 0)),
            pl.BlockSpec((None, 1, d), lambda i, j: (layer, 0, 0)),
            pl.BlockSpec((None, d, tf), lambda i, j: (layer, 0, j)),
            pl.BlockSpec((None, tf, d), lambda i, j: (layer, j, 0)),
            pl.BlockSpec((None, 1, d), lambda i, j: (layer, 0, 0)),
        ],
        out_specs=pl.BlockSpec((tm, d), lambda i, j: (i, 0)),
        out_shape=jax.ShapeDtypeStruct((t, d), F32),
        scratch_shapes=[pltpu.VMEM((tm, d), BF16)],
        compiler_params=_params(("arbitrary", "arbitrary")),
        name="mlp",
    )(x, g_pre, w1, w2, g_post)


def _tile_edges(seq_lens, tile):
    prev, nxt = [], []
    for s in seq_lens:
        assert s % tile == 0
        n = s // tile
        prev += [0] + [1] * (n - 1)
        nxt += [1] * (n - 1) + [0]
    return jnp.asarray(np.array([prev, nxt], dtype=np.int32))


def _rope_tables(seq_lens):
    max_len = max(seq_lens)
    inv_freq = ROPE_THETA ** (-jnp.arange(0, ROT_DIM, 2, dtype=F32) / ROT_DIM)
    ang = jnp.arange(max_len).astype(F32)[:, None] * inv_freq[None, :]
    cos, sin = jnp.cos(ang), jnp.sin(ang)
    rest = HEAD_DIM - ROT_DIM
    cos_t = jnp.concatenate([cos, cos, jnp.ones((max_len, rest), F32)], axis=-1)
    sin_t = jnp.concatenate([-sin, sin, jnp.zeros((max_len, rest), F32)], axis=-1)
    per_row = lambda tab: jnp.concatenate([tab[:n] for n in seq_lens], axis=0)
    return per_row(cos_t), per_row(sin_t)


def _trunk(groups, params, tiles):
    (g_pre_mix, w_in, b_gate, w_sink, w_conv, w_attn_out, w_conv_out, w_out,
     g_post_mix, g_pre_mlp, w_mlp_in, w_mlp_out, g_post_mlp) = params
    depth, d, _ = w_in.shape
    seq_lens = [g.shape[1] for g in groups for _ in range(g.shape[0])]
    x = jnp.concatenate([g.reshape(-1, d) for g in groups], axis=0)

    cos_rows, sin_rows = _rope_tables(seq_lens)
    attn_edges = _tile_edges(seq_lens, tiles.attn_rows)
    mix_edges = _tile_edges(seq_lens, tiles.mix_rows)

    kv0 = ATTN_WIDTH
    rest0 = ATTN_WIDTH + 2 * KV_WIDTH
    w_main = jnp.concatenate([w_in[:, :, :kv0], w_in[:, :, rest0:]], axis=2).astype(BF16)
    w_kv = w_in[:, :, kv0:rest0].astype(BF16)
    w_ao, w_co, w_o = (w.astype(BF16) for w in (w_attn_out, w_conv_out, w_out))
    w1, w2 = w_mlp_in.astype(BF16), w_mlp_out.astype(BF16)
    row = lambda p: p[:, None, :]

    for l in range(depth):
        proj, kv = _in_proj(l, x, row(g_pre_mix), w_main, w_kv, cos_rows, sin_rows, tiles)
        attn = _attn(proj, kv, w_sink[l], attn_edges, tiles)
        x = _mix_out(l, x, attn, proj, row(b_gate), w_conv, w_ao, w_co, w_o, row(g_post_mix),
                     mix_edges, tiles)
        x = _mlp(l, x, row(g_pre_mlp), w1, w2, row(g_post_mlp), tiles)

    outs, r0 = [], 0
    for g in groups:
        n = g.shape[0] * g.shape[1]
        outs.append(x[r0:r0 + n].reshape(g.shape))
        r0 += n
    return tuple(outs)


def kernel(x_prompt, x_sample, g_pre_mix, w_in, b_gate, w_sink, w_conv, w_attn_out, w_conv_out,
           w_out, g_post_mix, g_pre_mlp, w_mlp_in, w_mlp_out, g_post_mlp):
    params = (g_pre_mix, w_in, b_gate, w_sink, w_conv, w_attn_out, w_conv_out, w_out,
              g_post_mix, g_pre_mlp, w_mlp_in, w_mlp_out, g_post_mlp)
    return _trunk([x_prompt, x_sample], params, V7X_TILES)
```

```python
from typing import NamedTuple

import numpy as np
import jax
import jax.numpy as jnp
from jax import lax
from jax.experimental import pallas as pl
from jax.experimental.pallas import tpu as pltpu

F32 = jnp.float32
BF16 = jnp.bfloat16

HEAD_DIM = 128
N_Q_HEADS = 8
N_KV_HEADS = 2
GROUP = N_Q_HEADS // N_KV_HEADS
ATTN_WIDTH = N_Q_HEADS * HEAD_DIM
KV_WIDTH = N_KV_HEADS * HEAD_DIM
BLOCK = 128
ROT_DIM = HEAD_DIM // 4
ROT_HALF = ROT_DIM // 2
ROPE_THETA = 500000.0
RMS_EPS = 1e-6
NEG_INF = -1e30
LOG2_E = 1.4426950408889634
SUBLANES = 8
V7X_VMEM_LIMIT_BYTES = 60000 * 1024


class Tiles(NamedTuple):
    proj_rows: int
    proj_cols: int
    attn_rows: int
    mix_rows: int
    mlp_rows: int
    mlp_cols: int


V7X_TILES = Tiles(proj_rows=512, proj_cols=2048, attn_rows=1024, mix_rows=512,
                  mlp_rows=512, mlp_cols=1024)


def _rmsnorm(x, g):
    return x * lax.rsqrt(jnp.mean(x * x, axis=-1, keepdims=True) + RMS_EPS) * g


def _rope_head(xh, cos, sin, lane):
    partner = jnp.where(lane < ROT_HALF,
                        pltpu.roll(xh, HEAD_DIM - ROT_HALF, 1),
                        pltpu.roll(xh, ROT_HALF, 1))
    return jnp.where(lane < ROT_DIM, xh * cos + partner * sin, xh)


def _params(semantics):
    return pltpu.CompilerParams(dimension_semantics=semantics,
                                vmem_limit_bytes=V7X_VMEM_LIMIT_BYTES)


def _in_proj_kernel(x_ref, g_ref, w_ref, wkv_ref, cos_ref, sin_ref, proj_ref, kv_ref, h_ref):
    j = pl.program_id(1)
    rows = x_ref.shape[0]
    lane = lax.broadcasted_iota(jnp.int32, (rows, HEAD_DIM), 1)

    @pl.when(j == 0)
    def _():
        h_ref[...] = _rmsnorm(x_ref[...], g_ref[...]).astype(BF16)
        cos = cos_ref[...]
        sin = sin_ref[...]
        kv = jnp.dot(h_ref[...], wkv_ref[...], preferred_element_type=F32)
        for hd in range(N_KV_HEADS):
            sl = slice(hd * HEAD_DIM, (hd + 1) * HEAD_DIM)
            kv_ref[:, sl] = _rope_head(kv[:, sl], cos, sin, lane).astype(BF16)
        kv_ref[:, KV_WIDTH:] = kv[:, KV_WIDTH:].astype(BF16)
        q = jnp.dot(h_ref[...], w_ref[...], preferred_element_type=F32)
        for hd in range(N_Q_HEADS):
            sl = slice(hd * HEAD_DIM, (hd + 1) * HEAD_DIM)
            proj_ref[:, sl] = _rope_head(q[:, sl], cos, sin, lane).astype(BF16)
        if q.shape[1] > ATTN_WIDTH:
            proj_ref[:, ATTN_WIDTH:] = q[:, ATTN_WIDTH:].astype(BF16)

    @pl.when(j != 0)
    def _():
        proj_ref[...] = jnp.dot(h_ref[...], w_ref[...],
                                preferred_element_type=F32).astype(BF16)


def _in_proj(layer, x, g, w_main, w_kv, cos_rows, sin_rows, tiles):
    t, d = x.shape
    n_main = w_main.shape[2]
    tm, tn = tiles.proj_rows, tiles.proj_cols
    assert tn % ATTN_WIDTH == 0 and n_main % tn == 0 and t % tm == 0
    return pl.pallas_call(
        _in_proj_kernel,
        grid=(t // tm, n_main // tn),
        in_specs=[
            pl.BlockSpec((tm, d), lambda i, j: (i, 0)),
            pl.BlockSpec((None, 1, d), lambda i, j: (layer, 0, 0)),
            pl.BlockSpec((None, d, tn), lambda i, j: (layer, 0, j)),
            pl.BlockSpec((None, d, 2 * KV_WIDTH), lambda i, j: (layer, 0, 0),
                         pipeline_mode=pl.Buffered(1)),
            pl.BlockSpec((tm, HEAD_DIM), lambda i, j: (i, 0)),
            pl.BlockSpec((tm, HEAD_DIM), lambda i, j: (i, 0)),
        ],
        out_specs=[
            pl.BlockSpec((tm, tn), lambda i, j: (i, j)),
            pl.BlockSpec((tm, 2 * KV_WIDTH), lambda i, j: (i, 0)),
        ],
        scratch_shapes=[pltpu.VMEM((tm, d), BF16)],
        out_shape=[jax.ShapeDtypeStruct((t, n_main), BF16),
                   jax.ShapeDtypeStruct((t, 2 * KV_WIDTH), BF16)],
        compiler_params=_params(("arbitrary", "arbitrary")),
        name="in_proj",
    )(x, g, w_main, w_kv, cos_rows, sin_rows)


def _attn_kernel(edge_ref, sink_ref, q_ref, k_ref, v_ref, kp_ref, vp_ref, kn_ref, vn_ref,
                 o_ref, kbuf, vbuf):
    i = pl.program_id(0)
    rows = q_ref.shape[0]
    nblk = rows // BLOCK
    has_prev = edge_ref[0, i]
    has_next = edge_ref[1, i]
    kbuf[0:BLOCK, :] = kp_ref[...]
    kbuf[BLOCK:BLOCK + rows, :] = k_ref[...]
    kbuf[BLOCK + rows:, :] = kn_ref[...]
    vbuf[0:BLOCK, :] = vp_ref[...]
    vbuf[BLOCK:BLOCK + rows, :] = v_ref[...]
    vbuf[BLOCK + rows:, :] = vn_ref[...]

    r_io = lax.broadcasted_iota(jnp.int32, (BLOCK, BLOCK), 0)
    s_io = lax.broadcasted_iota(jnp.int32, (BLOCK, BLOCK), 1)
    scale = HEAD_DIM ** -0.5 * LOG2_E

    def block_body(b, carry):
        valid_prev = (s_io >= r_io) & jnp.logical_or(b > 0, has_prev == 1)
        valid_next = (s_io <= r_io) & jnp.logical_or(b < nblk - 1, has_next == 1)
        q0 = pl.multiple_of(b * BLOCK, BLOCK)
        for c in range(N_KV_HEADS):
            csl = slice(c * HEAD_DIM, (c + 1) * HEAD_DIM)
            kb = kbuf[pl.ds(q0, 3 * BLOCK), csl]
            vb = vbuf[pl.ds(q0, 3 * BLOCK), csl]
            qs = jnp.concatenate(
                [q_ref[pl.ds(q0, BLOCK), (c * GROUP + g) * HEAD_DIM:(c * GROUP + g + 1) * HEAD_DIM]
                 for g in range(GROUP)], axis=0)
            s = lax.dot_general(qs, kb, (((1,), (1,)), ((), ())),
                                preferred_element_type=F32) * scale
            ps, invs = [], []
            for g in range(GROUP):
                sink = sink_ref[c * GROUP + g] * LOG2_E
                sg = s[g * BLOCK:(g + 1) * BLOCK]
                sg = jnp.concatenate(
                    [jnp.where(valid_prev, sg[:, :BLOCK], NEG_INF),
                     sg[:, BLOCK:2 * BLOCK],
                     jnp.where(valid_next, sg[:, 2 * BLOCK:], NEG_INF)], axis=1)
                m = jnp.maximum(jnp.max(sg, axis=-1, keepdims=True), sink)
                p = jnp.exp2(sg - m)
                den = jnp.sum(p, axis=-1, keepdims=True) + jnp.exp2(sink - m)
                ps.append(p.astype(BF16))
                invs.append(1.0 / den)
            o = jnp.dot(jnp.concatenate(ps, axis=0), vb, preferred_element_type=F32)
            for g in range(GROUP):
                hsl = slice((c * GROUP + g) * HEAD_DIM, (c * GROUP + g + 1) * HEAD_DIM)
                o_ref[pl.ds(q0, BLOCK), hsl] = (o[g * BLOCK:(g + 1) * BLOCK] * invs[g]).astype(BF16)
        return carry

    lax.fori_loop(0, nblk, block_body, 0)


def _attn(proj, kv, sink, edges, tiles):
    t = proj.shape[0]
    tq = tiles.attn_rows
    assert t % tq == 0 and tq % BLOCK == 0
    per = tq // BLOCK
    last_blk = t // BLOCK - 1
    prev_map = lambda i, e: (jnp.maximum(i * per - 1, 0), 0)
    next_map = lambda i, e: (jnp.minimum((i + 1) * per, last_blk), 0)
    prev_map_v = lambda i, e: (jnp.maximum(i * per - 1, 0), 1)
    next_map_v = lambda i, e: (jnp.minimum((i + 1) * per, last_blk), 1)
    grid_spec = pltpu.PrefetchScalarGridSpec(
        num_scalar_prefetch=1,
        grid=(t // tq,),
        in_specs=[
            pl.BlockSpec(memory_space=pltpu.SMEM),
            pl.BlockSpec((tq, ATTN_WIDTH), lambda i, e: (i, 0)),
            pl.BlockSpec((tq, KV_WIDTH), lambda i, e: (i, 0)),
            pl.BlockSpec((tq, KV_WIDTH), lambda i, e: (i, 1)),
            pl.BlockSpec((BLOCK, KV_WIDTH), prev_map),
            pl.BlockSpec((BLOCK, KV_WIDTH), prev_map_v),
            pl.BlockSpec((BLOCK, KV_WIDTH), next_map),
            pl.BlockSpec((BLOCK, KV_WIDTH), next_map_v),
        ],
        out_specs=pl.BlockSpec((tq, ATTN_WIDTH), lambda i, e: (i, 0)),
        scratch_shapes=[pltpu.VMEM((tq + 2 * BLOCK, KV_WIDTH), BF16),
                        pltpu.VMEM((tq + 2 * BLOCK, KV_WIDTH), BF16)],
    )
    return pl.pallas_call(
        _attn_kernel,
        grid_spec=grid_spec,
        out_shape=jax.ShapeDtypeStruct((t, ATTN_WIDTH), BF16),
        compiler_params=_params(("arbitrary",)),
        name="attn",
    )(edges, sink, proj, kv, kv, kv, kv, kv, kv)


def _mix_out_kernel(edge_ref, attn_ref, b_ref, c_ref, xc_ref, cp_ref, xp_ref, cn_ref, xn_ref,
                    ga_ref, gc_ref, bgate_ref, wconv_ref, wao_ref, wco_ref, wout_ref,
                    x_ref, gpost_ref, o_ref):
    i = pl.program_id(0)
    rows, d = x_ref.shape

    u = c_ref[...].astype(F32) * xc_ref[...].astype(F32)
    up = (cp_ref[SUBLANES - 1:SUBLANES, :].astype(F32)
          * xp_ref[SUBLANES - 1:SUBLANES, :].astype(F32))
    un = cn_ref[0:1, :].astype(F32) * xn_ref[0:1, :].astype(F32)
    up = jnp.where(edge_ref[0, i] == 1, up, 0.0)
    un = jnp.where(edge_ref[1, i] == 1, un, 0.0)
    r_io = lax.broadcasted_iota(jnp.int32, u.shape, 0)
    u_prev = jnp.where(r_io == 0, up, pltpu.roll(u, 1, 0))
    u_next = jnp.where(r_io == rows - 1, un, pltpu.roll(u, rows - 1, 0))
    conv = u_prev * wconv_ref[0:1, :] + u * wconv_ref[1:2, :] + u_next * wconv_ref[2:3, :]
    conv = (b_ref[...].astype(F32) * conv).astype(BF16)

    o_a = jnp.dot(attn_ref[...], wao_ref[...], preferred_element_type=F32)
    o_c = jnp.dot(conv, wco_ref[...], preferred_element_type=F32)
    merged = (jax.nn.sigmoid(ga_ref[...].astype(F32) + bgate_ref[:, :d]) * o_a
              + jax.nn.sigmoid(gc_ref[...].astype(F32) + bgate_ref[:, d:]) * o_c)
    y = jnp.dot(merged.astype(BF16), wout_ref[...], preferred_element_type=F32)
    o_ref[...] = x_ref[...] + _rmsnorm(y, gpost_ref[...])


def _mix_out(layer, x, attn, proj, b_gate, w_conv, w_ao, w_co, w_out, g_post, edges, tiles):
    t, d = x.shape
    cw = w_co.shape[1]
    tm = tiles.mix_rows
    assert t % tm == 0 and tm % SUBLANES == 0
    assert ATTN_WIDTH == cw and (ATTN_WIDTH + 3 * cw) % d == 0
    ga_blk = (ATTN_WIDTH + 3 * cw) // d
    per8 = tm // SUBLANES
    last8 = t // SUBLANES - 1
    prev8 = lambda col: (lambda i, e: (jnp.maximum(i * per8 - 1, 0), col))
    next8 = lambda col: (lambda i, e: (jnp.minimum((i + 1) * per8, last8), col))
    resident = pl.Buffered(1)
    grid_spec = pltpu.PrefetchScalarGridSpec(
        num_scalar_prefetch=1,
        grid=(t // tm,),
        in_specs=[
            pl.BlockSpec((tm, ATTN_WIDTH), lambda i, e: (i, 0)),
            pl.BlockSpec((tm, cw), lambda i, e: (i, 1)),
            pl.BlockSpec((tm, cw), lambda i, e: (i, 2)),
            pl.BlockSpec((tm, cw), lambda i, e: (i, 3)),
            pl.BlockSpec((SUBLANES, cw), prev8(2)),
            pl.BlockSpec((SUBLANES, cw), prev8(3)),
            pl.BlockSpec((SUBLANES, cw), next8(2)),
            pl.BlockSpec((SUBLANES, cw), next8(3)),
            pl.BlockSpec((tm, d), lambda i, e: (i, ga_blk)),
            pl.BlockSpec((tm, d), lambda i, e: (i, ga_blk + 1)),
            pl.BlockSpec((None, 1, 2 * d), lambda i, e: (layer, 0, 0)),
            pl.BlockSpec((None, 3, cw), lambda i, e: (layer, 0, 0)),
            pl.BlockSpec((None, ATTN_WIDTH, d), lambda i, e: (layer, 0, 0), pipeline_mode=resident),
            pl.BlockSpec((None, cw, d), lambda i, e: (layer, 0, 0), pipeline_mode=resident),
            pl.BlockSpec((None, d, d), lambda i, e: (layer, 0, 0), pipeline_mode=resident),
            pl.BlockSpec((tm, d), lambda i, e: (i, 0)),
            pl.BlockSpec((None, 1, d), lambda i, e: (layer, 0, 0)),
        ],
        out_specs=pl.BlockSpec((tm, d), lambda i, e: (i, 0)),
    )
    return pl.pallas_call(
        _mix_out_kernel,
        grid_spec=grid_spec,
        out_shape=jax.ShapeDtypeStruct((t, d), F32),
        compiler_params=_params(("arbitrary",)),
        name="mix_out",
    )(edges, attn, proj, proj, proj, proj, proj, proj, proj, proj, proj,
      b_gate, w_conv, w_ao, w_co, w_out, x, g_post)


def _mlp_kernel(x_ref, gpre_ref, w1_ref, w2_ref, gpost_ref, o_ref, h_ref):
    j = pl.program_id(1)
    last = pl.num_programs(1) - 1

    def hidden_chunk():
        a = jnp.dot(h_ref[...], w1_ref[...], preferred_element_type=F32)
        a = jnp.square(jnp.maximum(a, 0.0)).astype(BF16)
        return jnp.dot(a, w2_ref[...], preferred_element_type=F32)

    @pl.when(j == 0)
    def _():
        h_ref[...] = _rmsnorm(x_ref[...], gpre_ref[...]).astype(BF16)
        o_ref[...] = hidden_chunk()

    @pl.when(jnp.logical_and(j > 0, j < last))
    def _():
        o_ref[...] += hidden_chunk()

    @pl.when(j == last)
    def _():
        f = o_ref[...] + hidden_chunk()
        o_ref[...] = x_ref[...] + _rmsnorm(f, gpost_ref[...])


def _mlp(layer, x, g_pre, w1, w2, g_post, tiles):
    t, d = x.shape
    f = w1.shape[2]
    tm, tf = tiles.mlp_rows, tiles.mlp_cols
    assert t % tm == 0 and f % tf == 0
    return pl.pallas_call(
        _mlp_kernel,
        grid=(t // tm, f // tf),
        in_specs=[
            pl.BlockSpec((tm, d), lambda i, j: (i, 0)),
            pl.BlockSpec((None, 1, d), lambda i, j: (layer, 0, 0)),
            pl.BlockSpec((None, d, tf), lambda i, j: (layer, 0, j)),
            pl.BlockSpec((None, tf, d), lambda i, j: (layer, j, 0)),
            pl.BlockSpec((None, 1, d), lambda i, j: (layer, 0, 0)),
        ],
        out_specs=pl.BlockSpec((tm, d), lambda i, j: (i, 0)),
        out_shape=jax.ShapeDtypeStruct((t, d), F32),
        scratch_shapes=[pltpu.VMEM((tm, d), BF16)],
        compiler_params=_params(("arbitrary", "arbitrary")),
        name="mlp",
    )(x, g_pre, w1, w2, g_post)


def _tile_edges(seq_lens, tile):
    prev, nxt = [], []
    for s in seq_lens:
        assert s % tile == 0
        n = s // tile
        prev += [0] + [1] * (n - 1)
        nxt += [1] * (n - 1) + [0]
    return jnp.asarray(np.array([prev, nxt], dtype=np.int32))


def _rope_tables(seq_lens):
    max_len = max(seq_lens)
    inv_freq = ROPE_THETA ** (-jnp.arange(0, ROT_DIM, 2, dtype=F32) / ROT_DIM)
    ang = jnp.arange(max_len).astype(F32)[:, None] * inv_freq[None, :]
    cos, sin = jnp.cos(ang), jnp.sin(ang)
    rest = HEAD_DIM - ROT_DIM
    cos_t = jnp.concatenate([cos, cos, jnp.ones((max_len, rest), F32)], axis=-1)
    sin_t = jnp.concatenate([-sin, sin, jnp.zeros((max_len, rest), F32)], axis=-1)
    per_row = lambda tab: jnp.concatenate([tab[:n] for n in seq_lens], axis=0)
    return per_row(cos_t), per_row(sin_t)


def _trunk(groups, params, tiles):
    (g_pre_mix, w_in, b_gate, w_sink, w_conv, w_attn_out, w_conv_out, w_out,
     g_post_mix, g_pre_mlp, w_mlp_in, w_mlp_out, g_post_mlp) = params
    depth, d, _ = w_in.shape
    seq_lens = [g.shape[1] for g in groups for _ in range(g.shape[0])]
    x = jnp.concatenate([g.reshape(-1, d) for g in groups], axis=0)

    cos_rows, sin_rows = _rope_tables(seq_lens)
    attn_edges = _tile_edges(seq_lens, tiles.attn_rows)
    mix_edges = _tile_edges(seq_lens, tiles.mix_rows)

    kv0 = ATTN_WIDTH
    rest0 = ATTN_WIDTH + 2 * KV_WIDTH
    w_main = jnp.concatenate([w_in[:, :, :kv0], w_in[:, :, rest0:]], axis=2).astype(BF16)
    w_kv = w_in[:, :, kv0:rest0].astype(BF16)
    w_ao, w_co, w_o = (w.astype(BF16) for w in (w_attn_out, w_conv_out, w_out))
    w1, w2 = w_mlp_in.astype(BF16), w_mlp_out.astype(BF16)
    row = lambda p: p[:, None, :]

    for l in range(depth):
        proj, kv = _in_proj(l, x, row(g_pre_mix), w_main, w_kv, cos_rows, sin_rows, tiles)
        attn = _attn(proj, kv, w_sink[l], attn_edges, tiles)
        x = _mix_out(l, x, attn, proj, row(b_gate), w_conv, w_ao, w_co, w_o, row(g_post_mix),
                     mix_edges, tiles)
        x = _mlp(l, x, row(g_pre_mlp), w1, w2, row(g_post_mlp), tiles)

    outs, r0 = [], 0
    for g in groups:
        n = g.shape[0] * g.shape[1]
        outs.append(x[r0:r0 + n].reshape(g.shape))
        r0 += n
    return tuple(outs)


def kernel(x_prompt, x_sample, g_pre_mix, w_in, b_gate, w_sink, w_conv, w_attn_out, w_conv_out,
           w_out, g_post_mix, g_pre_mlp, w_mlp_in, w_mlp_out, g_post_mlp):
    params = (g_pre_mix, w_in, b_gate, w_sink, w_conv, w_attn_out, w_conv_out, w_out,
              g_post_mix, g_pre_mlp, w_mlp_in, w_mlp_out, g_post_mlp)
    return _trunk([x_prompt, x_sample], params, V7X_TILES)
```

```python
import functools
from typing import NamedTuple

import numpy as np
import jax
import jax.numpy as jnp
from jax import lax
from jax.experimental import pallas as pl
from jax.experimental.pallas import tpu as pltpu

F32 = jnp.float32
BF16 = jnp.bfloat16

HEAD_DIM = 128
N_Q_HEADS = 8
N_KV_HEADS = 2
GROUP = N_Q_HEADS // N_KV_HEADS
ATTN_WIDTH = N_Q_HEADS * HEAD_DIM
KV_WIDTH = N_KV_HEADS * HEAD_DIM
BLOCK = 128
ROT_DIM = HEAD_DIM // 4
ROT_HALF = ROT_DIM // 2
ROPE_THETA = 500000.0
RMS_EPS = 1e-6
NEG_INF = -1e30
LOG2_E = 1.4426950408889634
SUBLANES = 8
V7X_VMEM_LIMIT_BYTES = 60000 * 1024


class Tiles(NamedTuple):
    proj_rows: int
    proj_cols: int
    mix_rows: int
    mlp_rows: int
    mlp_cols: int


V7X_TILES = Tiles(proj_rows=768, proj_cols=2048, mix_rows=512, mlp_rows=768, mlp_cols=1024)


def _rmsnorm(x, g):
    return x * lax.rsqrt(jnp.mean(x * x, axis=-1, keepdims=True) + RMS_EPS) * g


def _rope_head(xh, cos, sin, lane):
    partner = jnp.where(lane < ROT_HALF,
                        pltpu.roll(xh, HEAD_DIM - ROT_HALF, 1),
                        pltpu.roll(xh, ROT_HALF, 1))
    return jnp.where(lane < ROT_DIM, xh * cos + partner * sin, xh)


def _params(semantics):
    return pltpu.CompilerParams(dimension_semantics=semantics,
                                vmem_limit_bytes=V7X_VMEM_LIMIT_BYTES)


def _in_proj_kernel(x_ref, g_ref, w_ref, wkv_ref, cos_ref, sin_ref, proj_ref, kv_ref, h_ref):
    j = pl.program_id(1)
    rows = x_ref.shape[0]
    lane = lax.broadcasted_iota(jnp.int32, (rows, HEAD_DIM), 1)

    @pl.when(j == 0)
    def _():
        h_ref[...] = _rmsnorm(x_ref[...], g_ref[...]).astype(BF16)
        cos = cos_ref[...]
        sin = sin_ref[...]
        kv = jnp.dot(h_ref[...], wkv_ref[...], preferred_element_type=F32)
        for hd in range(N_KV_HEADS):
            sl = slice(hd * HEAD_DIM, (hd + 1) * HEAD_DIM)
            kv_ref[:, sl] = _rope_head(kv[:, sl], cos, sin, lane).astype(BF16)
        kv_ref[:, KV_WIDTH:] = kv[:, KV_WIDTH:].astype(BF16)
        q = jnp.dot(h_ref[...], w_ref[...], preferred_element_type=F32)
        for hd in range(N_Q_HEADS):
            sl = slice(hd * HEAD_DIM, (hd + 1) * HEAD_DIM)
            proj_ref[:, sl] = _rope_head(q[:, sl], cos, sin, lane).astype(BF16)
        if q.shape[1] > ATTN_WIDTH:
            proj_ref[:, ATTN_WIDTH:] = q[:, ATTN_WIDTH:].astype(BF16)

    @pl.when(j != 0)
    def _():
        proj_ref[...] = jnp.dot(h_ref[...], w_ref[...],
                                preferred_element_type=F32).astype(BF16)


def _in_proj(layer, x, g, w_main, w_kv, cos_rows, sin_rows, tiles):
    t, d = x.shape
    n_main = w_main.shape[2]
    tm, tn = tiles.proj_rows, tiles.proj_cols
    assert tn % ATTN_WIDTH == 0 and n_main % tn == 0 and t % tm == 0
    return pl.pallas_call(
        _in_proj_kernel,
        grid=(t // tm, n_main // tn),
        in_specs=[
            pl.BlockSpec((tm, d), lambda i, j: (i, 0)),
            pl.BlockSpec((None, 1, d), lambda i, j: (layer, 0, 0)),
            pl.BlockSpec((None, d, tn), lambda i, j: (layer, 0, j)),
            pl.BlockSpec((None, d, 2 * KV_WIDTH), lambda i, j: (layer, 0, 0),
                         pipeline_mode=pl.Buffered(1)),
            pl.BlockSpec((tm, HEAD_DIM), lambda i, j: (i, 0)),
            pl.BlockSpec((tm, HEAD_DIM), lambda i, j: (i, 0)),
        ],
        out_specs=[
            pl.BlockSpec((tm, tn), lambda i, j: (i, j)),
            pl.BlockSpec((tm, 2 * KV_WIDTH), lambda i, j: (i, 0)),
        ],
        scratch_shapes=[pltpu.VMEM((tm, d), BF16)],
        out_shape=[jax.ShapeDtypeStruct((t, n_main), BF16),
                   jax.ShapeDtypeStruct((t, 2 * KV_WIDTH), BF16)],
        compiler_params=_params(("arbitrary", "arbitrary")),
        name="in_proj",
    )(x, g, w_main, w_kv, cos_rows, sin_rows)


def _attention_tile(layer, has_prev, has_next, sink_ref, q_ref, k_ref, v_ref,
                    kp_ref, vp_ref, kn_ref, vn_ref, attn_ref):
    rows = q_ref.shape[0]
    nblk = rows // BLOCK
    r_io = lax.broadcasted_iota(jnp.int32, (BLOCK, BLOCK), 0)
    s_io = lax.broadcasted_iota(jnp.int32, (BLOCK, BLOCK), 1)
    tri_prev = s_io >= r_io
    tri_next = s_io <= r_io
    scale = HEAD_DIM ** -0.5 * LOG2_E

    def window(ref, halo_prev, halo_next, b, csl):
        prev = halo_prev[:, csl] if b == 0 else ref[(b - 1) * BLOCK:b * BLOCK, csl]
        nxt = halo_next[:, csl] if b == nblk - 1 else ref[(b + 1) * BLOCK:(b + 2) * BLOCK, csl]
        return jnp.concatenate([prev, ref[b * BLOCK:(b + 1) * BLOCK, csl], nxt], axis=0)

    for b in range(nblk):
        valid_prev = tri_prev if b > 0 else jnp.logical_and(tri_prev, has_prev)
        valid_next = tri_next if b < nblk - 1 else jnp.logical_and(tri_next, has_next)
        rsl = slice(b * BLOCK, (b + 1) * BLOCK)
        for c in range(N_KV_HEADS):
            csl = slice(c * HEAD_DIM, (c + 1) * HEAD_DIM)
            kb = window(k_ref, kp_ref, kn_ref, b, csl)
            vb = window(v_ref, vp_ref, vn_ref, b, csl)
            heads = [c * GROUP + g for g in range(GROUP)]
            qs = jnp.concatenate(
                [q_ref[rsl, h * HEAD_DIM:(h + 1) * HEAD_DIM] for h in heads], axis=0)
            s = lax.dot_general(qs, kb, (((1,), (1,)), ((), ())),
                                preferred_element_type=F32) * scale
            ps, invs = [], []
            for g, h in enumerate(heads):
                sink = sink_ref[layer, h] * LOG2_E
                sg = s[g * BLOCK:(g + 1) * BLOCK]
                sg = jnp.concatenate(
                    [jnp.where(valid_prev, sg[:, :BLOCK], NEG_INF),
                     sg[:, BLOCK:2 * BLOCK],
                     jnp.where(valid_next, sg[:, 2 * BLOCK:], NEG_INF)], axis=1)
                m = jnp.maximum(jnp.max(sg, axis=-1, keepdims=True), sink)
                p = jnp.exp2(sg - m)
                den = jnp.sum(p, axis=-1, keepdims=True) + jnp.exp2(sink - m)
                ps.append(p.astype(BF16))
                invs.append(1.0 / den)
            o = jnp.dot(jnp.concatenate(ps, axis=0), vb, preferred_element_type=F32)
            for g, h in enumerate(heads):
                attn_ref[rsl, h * HEAD_DIM:(h + 1) * HEAD_DIM] = (
                    o[g * BLOCK:(g + 1) * BLOCK] * invs[g]).astype(BF16)


def _short_conv_tile(has_prev, has_next, b_ref, c_ref, xc_ref, cp_ref, xp_ref, cn_ref, xn_ref,
                     wconv_ref):
    rows = c_ref.shape[0]
    u = c_ref[...].astype(F32) * xc_ref[...].astype(F32)
    up = (cp_ref[SUBLANES - 1:SUBLANES, :].astype(F32)
          * xp_ref[SUBLANES - 1:SUBLANES, :].astype(F32))
    un = cn_ref[0:1, :].astype(F32) * xn_ref[0:1, :].astype(F32)
    up = jnp.where(has_prev, up, 0.0)
    un = jnp.where(has_next, un, 0.0)
    r_io = lax.broadcasted_iota(jnp.int32, u.shape, 0)
    u_prev = jnp.where(r_io == 0, up, pltpu.roll(u, 1, 0))
    u_next = jnp.where(r_io == rows - 1, un, pltpu.roll(u, rows - 1, 0))
    conv = u_prev * wconv_ref[0:1, :] + u * wconv_ref[1:2, :] + u_next * wconv_ref[2:3, :]
    return (b_ref[...].astype(F32) * conv).astype(BF16)


def _mixer_kernel(layer, edge_ref, sink_ref, q_ref, k_ref, v_ref, kp_ref, vp_ref, kn_ref, vn_ref,
                  b_ref, c_ref, xc_ref, cp_ref, xp_ref, cn_ref, xn_ref,
                  ga_ref, gc_ref, bgate_ref, wconv_ref, wao_ref, wco_ref, wout_ref,
                  x_ref, gpost_ref, o_ref, attn_ref):
    i = pl.program_id(0)
    d = x_ref.shape[1]
    has_prev = edge_ref[0, i] == 1
    has_next = edge_ref[1, i] == 1

    _attention_tile(layer, has_prev, has_next, sink_ref, q_ref, k_ref, v_ref,
                    kp_ref, vp_ref, kn_ref, vn_ref, attn_ref)
    conv = _short_conv_tile(has_prev, has_next, b_ref, c_ref, xc_ref, cp_ref, xp_ref,
                            cn_ref, xn_ref, wconv_ref)

    o_a = jnp.dot(attn_ref[...], wao_ref[...], preferred_element_type=F32)
    o_c = jnp.dot(conv, wco_ref[...], preferred_element_type=F32)
    merged = (jax.nn.sigmoid(ga_ref[...].astype(F32) + bgate_ref[:, :d]) * o_a
              + jax.nn.sigmoid(gc_ref[...].astype(F32) + bgate_ref[:, d:]) * o_c)
    y = jnp.dot(merged.astype(BF16), wout_ref[...], preferred_element_type=F32)
    o_ref[...] = x_ref[...] + _rmsnorm(y, gpost_ref[...])


def _mixer(layer, x, proj, kv, sink, b_gate, w_conv, w_ao, w_co, w_out, g_post, edges, tiles):
    t, d = x.shape
    cw = w_co.shape[1]
    tm = tiles.mix_rows
    assert t % tm == 0 and tm % BLOCK == 0
    assert ATTN_WIDTH == cw and (ATTN_WIDTH + 3 * cw) % d == 0
    ga_blk = (ATTN_WIDTH + 3 * cw) // d
    per8, last8 = tm // SUBLANES, t // SUBLANES - 1
    perb, lastb = tm // BLOCK, t // BLOCK - 1
    row_before = lambda per, col: (lambda i, e: (jnp.maximum(i * per - 1, 0), col))
    row_after = lambda per, last, col: (lambda i, e: (jnp.minimum((i + 1) * per, last), col))
    resident = pl.Buffered(1)
    grid_spec = pltpu.PrefetchScalarGridSpec(
        num_scalar_prefetch=1,
        grid=(t // tm,),
        in_specs=[
            pl.BlockSpec(memory_space=pltpu.SMEM),
            pl.BlockSpec((tm, ATTN_WIDTH), lambda i, e: (i, 0)),
            pl.BlockSpec((tm, KV_WIDTH), lambda i, e: (i, 0)),
            pl.BlockSpec((tm, KV_WIDTH), lambda i, e: (i, 1)),
            pl.BlockSpec((BLOCK, KV_WIDTH), row_before(perb, 0)),
            pl.BlockSpec((BLOCK, KV_WIDTH), row_before(perb, 1)),
            pl.BlockSpec((BLOCK, KV_WIDTH), row_after(perb, lastb, 0)),
            pl.BlockSpec((BLOCK, KV_WIDTH), row_after(perb, lastb, 1)),
            pl.BlockSpec((tm, cw), lambda i, e: (i, 1)),
            pl.BlockSpec((tm, cw), lambda i, e: (i, 2)),
            pl.BlockSpec((tm, cw), lambda i, e: (i, 3)),
            pl.BlockSpec((SUBLANES, cw), row_before(per8, 2)),
            pl.BlockSpec((SUBLANES, cw), row_before(per8, 3)),
            pl.BlockSpec((SUBLANES, cw), row_after(per8, last8, 2)),
            pl.BlockSpec((SUBLANES, cw), row_after(per8, last8, 3)),
            pl.BlockSpec((tm, d), lambda i, e: (i, ga_blk)),
            pl.BlockSpec((tm, d), lambda i, e: (i, ga_blk + 1)),
            pl.BlockSpec((None, 1, 2 * d), lambda i, e: (layer, 0, 0)),
            pl.BlockSpec((None, 3, cw), lambda i, e: (layer, 0, 0)),
            pl.BlockSpec((None, ATTN_WIDTH, d), lambda i, e: (layer, 0, 0), pipeline_mode=resident),
            pl.BlockSpec((None, cw, d), lambda i, e: (layer, 0, 0), pipeline_mode=resident),
            pl.BlockSpec((None, d, d), lambda i, e: (layer, 0, 0), pipeline_mode=resident),
            pl.BlockSpec((tm, d), lambda i, e: (i, 0)),
            pl.BlockSpec((None, 1, d), lambda i, e: (layer, 0, 0)),
        ],
        out_specs=pl.BlockSpec((tm, d), lambda i, e: (i, 0)),
        scratch_shapes=[pltpu.VMEM((tm, ATTN_WIDTH), BF16)],
    )
    return pl.pallas_call(
        functools.partial(_mixer_kernel, layer),
        grid_spec=grid_spec,
        out_shape=jax.ShapeDtypeStruct((t, d), F32),
        compiler_params=_params(("arbitrary",)),
        name="mixer",
    )(edges, sink, proj, kv, kv, kv, kv, kv, kv, proj, proj, proj, proj, proj, proj, proj,
      proj, proj, b_gate, w_conv, w_ao, w_co, w_out, x, g_post)


def _mlp_kernel(x_ref, gpre_ref, w1_ref, w2_ref, gpost_ref, o_ref, h_ref):
    j = pl.program_id(1)
    last = pl.num_programs(1) - 1

    def hidden_chunk():
        a = jnp.dot(h_ref[...], w1_ref[...], preferred_element_type=F32)
        a = jnp.square(jnp.maximum(a, 0.0)).astype(BF16)
        return jnp.dot(a, w2_ref[...], preferred_element_type=F32)

    @pl.when(j == 0)
    def _():
        h_ref[...] = _rmsnorm(x_ref[...], gpre_ref[...]).astype(BF16)
        o_ref[...] = hidden_chunk()

    @pl.when(jnp.logical_and(j > 0, j < last))
    def _():
        o_ref[...] += hidden_chunk()

    @pl.when(j == last)
    def _():
        f = o_ref[...] + hidden_chunk()
        o_ref[...] = x_ref[...] + _rmsnorm(f, gpost_ref[...])


def _mlp(layer, x, g_pre, w1, w2, g_post, tiles):
    t, d = x.shape
    f = w1.shape[2]
    tm, tf = tiles.mlp_rows, tiles.mlp_cols
    assert t % tm == 0 and f % tf == 0
    return pl.pallas_call(
        _mlp_kernel,
        grid=(t // tm, f // tf),
        in_specs=[
            pl.BlockSpec((tm, d), lambda i, j: (i, 0)),
            pl.BlockSpec((None, 1, d), lambda i, j: (layer, 0, 0)),
            pl.BlockSpec((None, d, tf), lambda i, j: (layer, 0, j)),
            pl.BlockSpec((None, tf, d), lambda i, j: (layer, j, 0)),
            pl.BlockSpec((None, 1, d), lambda i, j: (layer, 0, 0)),
        ],
        out_specs=pl.BlockSpec((tm, d), lambda i, j: (i, 0)),
        out_shape=jax.ShapeDtypeStruct((t, d), F32),
        scratch_shapes=[pltpu.VMEM((tm, d), BF16)],
        compiler_params=_params(("arbitrary", "arbitrary")),
        name="mlp",
    )(x, g_pre, w1, w2, g_post)


def _tile_edges(seq_lens, tile):
    prev, nxt = [], []
    for s in seq_lens:
        assert s % tile == 0
        n = s // tile
        prev += [0] + [1] * (n - 1)
        nxt += [1] * (n - 1) + [0]
    return jnp.asarray(np.array([prev, nxt], dtype=np.int32))


def _rope_tables(seq_lens):
    max_len = max(seq_lens)
    inv_freq = ROPE_THETA ** (-jnp.arange(0, ROT_DIM, 2, dtype=F32) / ROT_DIM)
    ang = jnp.arange(max_len).astype(F32)[:, None] * inv_freq[None, :]
    cos, sin = jnp.cos(ang), jnp.sin(ang)
    rest = HEAD_DIM - ROT_DIM
    cos_t = jnp.concatenate([cos, cos, jnp.ones((max_len, rest), F32)], axis=-1)
    sin_t = jnp.concatenate([-sin, sin, jnp.zeros((max_len, rest), F32)], axis=-1)
    per_row = lambda tab: jnp.concatenate([tab[:n] for n in seq_lens], axis=0)
    return per_row(cos_t), per_row(sin_t)


def _trunk(groups, params, tiles):
    (g_pre_mix, w_in, b_gate, w_sink, w_conv, w_attn_out, w_conv_out, w_out,
     g_post_mix, g_pre_mlp, w_mlp_in, w_mlp_out, g_post_mlp) = params
    depth, d, _ = w_in.shape
    seq_lens = [g.shape[1] for g in groups for _ in range(g.shape[0])]
    x = jnp.concatenate([g.reshape(-1, d) for g in groups], axis=0)

    cos_rows, sin_rows = _rope_tables(seq_lens)
    mix_edges = _tile_edges(seq_lens, tiles.mix_rows)

    kv0 = ATTN_WIDTH
    rest0 = ATTN_WIDTH + 2 * KV_WIDTH
    w_main = jnp.concatenate([w_in[:, :, :kv0], w_in[:, :, rest0:]], axis=2).astype(BF16)
    w_kv = w_in[:, :, kv0:rest0].astype(BF16)
    w_ao, w_co, w_o = (w.astype(BF16) for w in (w_attn_out, w_conv_out, w_out))
    w1, w2 = w_mlp_in.astype(BF16), w_mlp_out.astype(BF16)
    row = lambda p: p[:, None, :]

    for l in range(depth):
        proj, kv = _in_proj(l, x, row(g_pre_mix), w_main, w_kv, cos_rows, sin_rows, tiles)
        x = _mixer(l, x, proj, kv, w_sink, row(b_gate), w_conv, w_ao, w_co, w_o,
                   row(g_post_mix), mix_edges, tiles)
        x = _mlp(l, x, row(g_pre_mlp), w1, w2, row(g_post_mlp), tiles)

    outs, r0 = [], 0
    for g in groups:
        n = g.shape[0] * g.shape[1]
        outs.append(x[r0:r0 + n].reshape(g.shape))
        r0 += n
    return tuple(outs)


def kernel(x_prompt, x_sample, g_pre_mix, w_in, b_gate, w_sink, w_conv, w_attn_out, w_conv_out,
           w_out, g_post_mix, g_pre_mlp, w_mlp_in, w_mlp_out, g_post_mlp):
    params = (g_pre_mix, w_in, b_gate, w_sink, w_conv, w_attn_out, w_conv_out, w_out,
              g_post_mix, g_pre_mlp, w_mlp_in, w_mlp_out, g_post_mlp)
    return _trunk([x_prompt, x_sample], params, V7X_TILES)
```

```python
import functools
from typing import NamedTuple

import numpy as np
import jax
import jax.numpy as jnp
from jax import lax
from jax.experimental import pallas as pl
from jax.experimental.pallas import tpu as pltpu

F32 = jnp.float32
BF16 = jnp.bfloat16

HEAD_DIM = 128
N_Q_HEADS = 8
N_KV_HEADS = 2
GROUP = N_Q_HEADS // N_KV_HEADS
ATTN_WIDTH = N_Q_HEADS * HEAD_DIM
KV_WIDTH = N_KV_HEADS * HEAD_DIM
BLOCK = 128
ROT_DIM = HEAD_DIM // 4
ROT_HALF = ROT_DIM // 2
ROPE_THETA = 500000.0
RMS_EPS = 1e-6
NEG_INF = -1e30
LOG2_E = 1.4426950408889634
SUBLANES = 8
V7X_VMEM_LIMIT_BYTES = 60000 * 1024


class Tiles(NamedTuple):
    proj_rows: int
    proj_cols: int
    mix_rows: int
    mlp_rows: int
    mlp_cols: int
    out_rows: int


V7X_TILES = Tiles(proj_rows=768, proj_cols=2048, mix_rows=512, mlp_rows=768, mlp_cols=1024,
                  out_rows=512)


def _rmsnorm(x, g):
    return x * lax.rsqrt(jnp.mean(x * x, axis=-1, keepdims=True) + RMS_EPS) * g


def _rope_head(xh, cos, sin, lane):
    partner = jnp.where(lane < ROT_HALF,
                        pltpu.roll(xh, HEAD_DIM - ROT_HALF, 1),
                        pltpu.roll(xh, ROT_HALF, 1))
    return jnp.where(lane < ROT_DIM, xh * cos + partner * sin, xh)


def _params(semantics):
    return pltpu.CompilerParams(dimension_semantics=semantics,
                                vmem_limit_bytes=V7X_VMEM_LIMIT_BYTES)


def _in_proj_kernel(x_ref, g_ref, w_ref, wkv_ref, cos_ref, sin_ref, proj_ref, kv_ref, h_ref):
    j = pl.program_id(1)
    rows = x_ref.shape[0]
    lane = lax.broadcasted_iota(jnp.int32, (rows, HEAD_DIM), 1)

    @pl.when(j == 0)
    def _():
        h_ref[...] = _rmsnorm(x_ref[...], g_ref[...]).astype(BF16)
        cos = cos_ref[...]
        sin = sin_ref[...]
        kv = jnp.dot(h_ref[...], wkv_ref[...], preferred_element_type=F32)
        for hd in range(N_KV_HEADS):
            sl = slice(hd * HEAD_DIM, (hd + 1) * HEAD_DIM)
            kv_ref[:, sl] = _rope_head(kv[:, sl], cos, sin, lane).astype(BF16)
        kv_ref[:, KV_WIDTH:] = kv[:, KV_WIDTH:].astype(BF16)
        q = jnp.dot(h_ref[...], w_ref[...], preferred_element_type=F32)
        for hd in range(N_Q_HEADS):
            sl = slice(hd * HEAD_DIM, (hd + 1) * HEAD_DIM)
            proj_ref[:, sl] = _rope_head(q[:, sl], cos, sin, lane).astype(BF16)
        if q.shape[1] > ATTN_WIDTH:
            proj_ref[:, ATTN_WIDTH:] = q[:, ATTN_WIDTH:].astype(BF16)

    @pl.when(j != 0)
    def _():
        proj_ref[...] = jnp.dot(h_ref[...], w_ref[...],
                                preferred_element_type=F32).astype(BF16)


def _in_proj(layer, x, g, w_main, w_kv, cos_rows, sin_rows, tiles):
    t, d = x.shape
    n_main = w_main.shape[2]
    tm, tn = tiles.proj_rows, tiles.proj_cols
    assert tn % ATTN_WIDTH == 0 and n_main % tn == 0 and t % tm == 0
    return pl.pallas_call(
        _in_proj_kernel,
        grid=(t // tm, n_main // tn),
        in_specs=[
            pl.BlockSpec((tm, d), lambda i, j: (i, 0)),
            pl.BlockSpec((None, 1, d), lambda i, j: (layer, 0, 0)),
            pl.BlockSpec((None, d, tn), lambda i, j: (layer, 0, j)),
            pl.BlockSpec((None, d, 2 * KV_WIDTH), lambda i, j: (layer, 0, 0),
                         pipeline_mode=pl.Buffered(1)),
            pl.BlockSpec((tm, HEAD_DIM), lambda i, j: (i, 0)),
            pl.BlockSpec((tm, HEAD_DIM), lambda i, j: (i, 0)),
        ],
        out_specs=[
            pl.BlockSpec((tm, tn), lambda i, j: (i, j)),
            pl.BlockSpec((tm, 2 * KV_WIDTH), lambda i, j: (i, 0)),
        ],
        scratch_shapes=[pltpu.VMEM((tm, d), BF16)],
        out_shape=[jax.ShapeDtypeStruct((t, n_main), BF16),
                   jax.ShapeDtypeStruct((t, 2 * KV_WIDTH), BF16)],
        compiler_params=_params(("arbitrary", "arbitrary")),
        name="in_proj",
    )(x, g, w_main, w_kv, cos_rows, sin_rows)


def _attention_tile(layer, has_prev, has_next, sink_ref, q_ref, k_ref, v_ref,
                    kp_ref, vp_ref, kn_ref, vn_ref, attn_ref):
    rows = q_ref.shape[0]
    nblk = rows // BLOCK
    r_io = lax.broadcasted_iota(jnp.int32, (BLOCK, BLOCK), 0)
    s_io = lax.broadcasted_iota(jnp.int32, (BLOCK, BLOCK), 1)
    tri_prev = s_io >= r_io
    tri_next = s_io <= r_io
    scale = HEAD_DIM ** -0.5 * LOG2_E

    def window(ref, halo_prev, halo_next, b, csl):
        prev = halo_prev[:, csl] if b == 0 else ref[(b - 1) * BLOCK:b * BLOCK, csl]
        nxt = halo_next[:, csl] if b == nblk - 1 else ref[(b + 1) * BLOCK:(b + 2) * BLOCK, csl]
        return jnp.concatenate([prev, ref[b * BLOCK:(b + 1) * BLOCK, csl], nxt], axis=0)

    for b in range(nblk):
        valid_prev = tri_prev if b > 0 else jnp.logical_and(tri_prev, has_prev)
        valid_next = tri_next if b < nblk - 1 else jnp.logical_and(tri_next, has_next)
        rsl = slice(b * BLOCK, (b + 1) * BLOCK)
        for c in range(N_KV_HEADS):
            csl = slice(c * HEAD_DIM, (c + 1) * HEAD_DIM)
            kb = window(k_ref, kp_ref, kn_ref, b, csl)
            vb = window(v_ref, vp_ref, vn_ref, b, csl)
            heads = [c * GROUP + g for g in range(GROUP)]
            qs = jnp.concatenate(
                [q_ref[rsl, h * HEAD_DIM:(h + 1) * HEAD_DIM] for h in heads], axis=0)
            s = lax.dot_general(qs, kb, (((1,), (1,)), ((), ())),
                                preferred_element_type=F32) * scale
            ps, invs = [], []
            for g, h in enumerate(heads):
                sink = sink_ref[layer, h] * LOG2_E
                sg = s[g * BLOCK:(g + 1) * BLOCK]
                sg = jnp.concatenate(
                    [jnp.where(valid_prev, sg[:, :BLOCK], NEG_INF),
                     sg[:, BLOCK:2 * BLOCK],
                     jnp.where(valid_next, sg[:, 2 * BLOCK:], NEG_INF)], axis=1)
                m = jnp.maximum(jnp.max(sg, axis=-1, keepdims=True), sink)
                p = jnp.exp2(sg - m)
                den = jnp.sum(p, axis=-1, keepdims=True) + jnp.exp2(sink - m)
                ps.append(p.astype(BF16))
                invs.append(1.0 / den)
            o = jnp.dot(jnp.concatenate(ps, axis=0), vb, preferred_element_type=F32)
            for g, h in enumerate(heads):
                attn_ref[rsl, h * HEAD_DIM:(h + 1) * HEAD_DIM] = (
                    o[g * BLOCK:(g + 1) * BLOCK] * invs[g]).astype(BF16)


def _short_conv_tile(has_prev, has_next, b_ref, c_ref, xc_ref, cp_ref, xp_ref, cn_ref, xn_ref,
                     wconv_ref):
    rows = c_ref.shape[0]
    u = c_ref[...].astype(F32) * xc_ref[...].astype(F32)
    up = (cp_ref[SUBLANES - 1:SUBLANES, :].astype(F32)
          * xp_ref[SUBLANES - 1:SUBLANES, :].astype(F32))
    un = cn_ref[0:1, :].astype(F32) * xn_ref[0:1, :].astype(F32)
    up = jnp.where(has_prev, up, 0.0)
    un = jnp.where(has_next, un, 0.0)
    r_io = lax.broadcasted_iota(jnp.int32, u.shape, 0)
    u_prev = jnp.where(r_io == 0, up, pltpu.roll(u, 1, 0))
    u_next = jnp.where(r_io == rows - 1, un, pltpu.roll(u, rows - 1, 0))
    conv = u_prev * wconv_ref[0:1, :] + u * wconv_ref[1:2, :] + u_next * wconv_ref[2:3, :]
    return (b_ref[...].astype(F32) * conv).astype(BF16)


def _mixer_kernel(layer, edge_ref, sink_ref, q_ref, k_ref, v_ref, kp_ref, vp_ref, kn_ref, vn_ref,
                  b_ref, c_ref, xc_ref, cp_ref, xp_ref, cn_ref, xn_ref,
                  ga_ref, gc_ref, bgate_ref, wconv_ref, wao_ref, wco_ref, wout_ref,
                  x_ref, gpost_ref, o_ref, attn_ref):
    i = pl.program_id(0)
    d = x_ref.shape[1]
    has_prev = edge_ref[0, i] == 1
    has_next = edge_ref[1, i] == 1

    _attention_tile(layer, has_prev, has_next, sink_ref, q_ref, k_ref, v_ref,
                    kp_ref, vp_ref, kn_ref, vn_ref, attn_ref)
    conv = _short_conv_tile(has_prev, has_next, b_ref, c_ref, xc_ref, cp_ref, xp_ref,
                            cn_ref, xn_ref, wconv_ref)

    o_a = jnp.dot(attn_ref[...], wao_ref[...], preferred_element_type=F32)
    o_c = jnp.dot(conv, wco_ref[...], preferred_element_type=F32)
    merged = (jax.nn.sigmoid(ga_ref[...].astype(F32) + bgate_ref[:, :d]) * o_a
              + jax.nn.sigmoid(gc_ref[...].astype(F32) + bgate_ref[:, d:]) * o_c)
    y = jnp.dot(merged.astype(BF16), wout_ref[...], preferred_element_type=F32)
    o_ref[...] = x_ref[...] + _rmsnorm(y, gpost_ref[...])


def _mixer(layer, x, proj, kv, sink, b_gate, w_conv, w_ao, w_co, w_out, g_post, edges, tiles):
    t, d = x.shape
    cw = w_co.shape[1]
    tm = tiles.mix_rows
    assert t % tm == 0 and tm % BLOCK == 0
    assert ATTN_WIDTH == cw and (ATTN_WIDTH + 3 * cw) % d == 0
    ga_blk = (ATTN_WIDTH + 3 * cw) // d
    per8, last8 = tm // SUBLANES, t // SUBLANES - 1
    perb, lastb = tm // BLOCK, t // BLOCK - 1
    row_before = lambda per, col: (lambda i, e: (jnp.maximum(i * per - 1, 0), col))
    row_after = lambda per, last, col: (lambda i, e: (jnp.minimum((i + 1) * per, last), col))
    resident = pl.Buffered(1)
    grid_spec = pltpu.PrefetchScalarGridSpec(
        num_scalar_prefetch=1,
        grid=(t // tm,),
        in_specs=[
            pl.BlockSpec(memory_space=pltpu.SMEM),
            pl.BlockSpec((tm, ATTN_WIDTH), lambda i, e: (i, 0)),
            pl.BlockSpec((tm, KV_WIDTH), lambda i, e: (i, 0)),
            pl.BlockSpec((tm, KV_WIDTH), lambda i, e: (i, 1)),
            pl.BlockSpec((BLOCK, KV_WIDTH), row_before(perb, 0)),
            pl.BlockSpec((BLOCK, KV_WIDTH), row_before(perb, 1)),
            pl.BlockSpec((BLOCK, KV_WIDTH), row_after(perb, lastb, 0)),
            pl.BlockSpec((BLOCK, KV_WIDTH), row_after(perb, lastb, 1)),
            pl.BlockSpec((tm, cw), lambda i, e: (i, 1)),
            pl.BlockSpec((tm, cw), lambda i, e: (i, 2)),
            pl.BlockSpec((tm, cw), lambda i, e: (i, 3)),
            pl.BlockSpec((SUBLANES, cw), row_before(per8, 2)),
            pl.BlockSpec((SUBLANES, cw), row_before(per8, 3)),
            pl.BlockSpec((SUBLANES, cw), row_after(per8, last8, 2)),
            pl.BlockSpec((SUBLANES, cw), row_after(per8, last8, 3)),
            pl.BlockSpec((tm, d), lambda i, e: (i, ga_blk)),
            pl.BlockSpec((tm, d), lambda i, e: (i, ga_blk + 1)),
            pl.BlockSpec((None, 1, 2 * d), lambda i, e: (layer, 0, 0)),
            pl.BlockSpec((None, 3, cw), lambda i, e: (layer, 0, 0)),
            pl.BlockSpec((None, ATTN_WIDTH, d), lambda i, e: (layer, 0, 0), pipeline_mode=resident),
            pl.BlockSpec((None, cw, d), lambda i, e: (layer, 0, 0), pipeline_mode=resident),
            pl.BlockSpec((None, d, d), lambda i, e: (layer, 0, 0), pipeline_mode=resident),
            pl.BlockSpec((tm, d), lambda i, e: (i, 0)),
            pl.BlockSpec((None, 1, d), lambda i, e: (layer, 0, 0)),
        ],
        out_specs=pl.BlockSpec((tm, d), lambda i, e: (i, 0)),
        scratch_shapes=[pltpu.VMEM((tm, ATTN_WIDTH), BF16)],
    )
    return pl.pallas_call(
        functools.partial(_mixer_kernel, layer),
        grid_spec=grid_spec,
        out_shape=jax.ShapeDtypeStruct((t, d), F32),
        compiler_params=_params(("arbitrary",)),
        name="mixer",
    )(edges, sink, proj, kv, kv, kv, kv, kv, kv, proj, proj, proj, proj, proj, proj, proj,
      proj, proj, b_gate, w_conv, w_ao, w_co, w_out, x, g_post)


def _mlp_kernel(x_ref, gpre_ref, w1_ref, w2_ref, gpost_ref, o_ref, h_ref):
    j = pl.program_id(1)
    last = pl.num_programs(1) - 1

    def hidden_chunk():
        a = jnp.dot(h_ref[...], w1_ref[...], preferred_element_type=F32)
        a = jnp.square(jnp.maximum(a, 0.0)).astype(BF16)
        return jnp.dot(a, w2_ref[...], preferred_element_type=F32)

    @pl.when(j == 0)
    def _():
        h_ref[...] = _rmsnorm(x_ref[...], gpre_ref[...]).astype(BF16)
        o_ref[...] = hidden_chunk()

    @pl.when(jnp.logical_and(j > 0, j < last))
    def _():
        o_ref[...] += hidden_chunk()

    @pl.when(j == last)
    def _():
        f = o_ref[...] + hidden_chunk()
        o_ref[...] = x_ref[...] + _rmsnorm(f, gpost_ref[...])


def _mlp(layer, x, g_pre, w1, w2, g_post, tm, tf, row0=0, rows=None):
    d = x.shape[1]
    t = x.shape[0] if rows is None else rows
    f = w1.shape[2]
    assert t % tm == 0 and f % tf == 0 and row0 % tm == 0
    blk0 = row0 // tm
    return pl.pallas_call(
        _mlp_kernel,
        grid=(t // tm, f // tf),
        in_specs=[
            pl.BlockSpec((tm, d), lambda i, j: (blk0 + i, 0)),
            pl.BlockSpec((None, 1, d), lambda i, j: (layer, 0, 0)),
            pl.BlockSpec((None, d, tf), lambda i, j: (layer, 0, j)),
            pl.BlockSpec((None, tf, d), lambda i, j: (layer, j, 0)),
            pl.BlockSpec((None, 1, d), lambda i, j: (layer, 0, 0)),
        ],
        out_specs=pl.BlockSpec((tm, d), lambda i, j: (i, 0)),
        out_shape=jax.ShapeDtypeStruct((t, d), F32),
        scratch_shapes=[pltpu.VMEM((tm, d), BF16)],
        compiler_params=_params(("arbitrary", "arbitrary")),
        name="mlp",
    )(x, g_pre, w1, w2, g_post)


def _cast_kernel(w_ref, o_ref):
    o_ref[...] = w_ref[...].astype(o_ref.dtype)


def _regroup_w_in(w_in):
    depth, d, n = w_in.shape
    cb = 2 * KV_WIDTH
    assert ATTN_WIDTH % cb == 0 and n % cb == 0
    q_blocks = ATTN_WIDTH // cb
    n_main = n - cb
    return pl.pallas_call(
        _cast_kernel,
        grid=(depth, n_main // cb),
        in_specs=[pl.BlockSpec((None, d, cb),
                               lambda l, j: (l, 0, jnp.where(j < q_blocks, j, j + 1)))],
        out_specs=pl.BlockSpec((None, d, cb), lambda l, j: (l, 0, j)),
        out_shape=jax.ShapeDtypeStruct((depth, d, n_main), BF16),
        compiler_params=_params(("arbitrary", "arbitrary")),
        name="regroup_w_in",
    )(w_in)


def _tile_edges(seq_lens, tile):
    prev, nxt = [], []
    for s in seq_lens:
        assert s % tile == 0
        n = s // tile
        prev += [0] + [1] * (n - 1)
        nxt += [1] * (n - 1) + [0]
    return jnp.asarray(np.array([prev, nxt], dtype=np.int32))


def _rope_tables(seq_lens):
    max_len = max(seq_lens)
    inv_freq = ROPE_THETA ** (-jnp.arange(0, ROT_DIM, 2, dtype=F32) / ROT_DIM)
    ang = jnp.arange(max_len).astype(F32)[:, None] * inv_freq[None, :]
    cos, sin = jnp.cos(ang), jnp.sin(ang)
    rest = HEAD_DIM - ROT_DIM
    cos_t = jnp.concatenate([cos, cos, jnp.ones((max_len, rest), F32)], axis=-1)
    sin_t = jnp.concatenate([-sin, sin, jnp.zeros((max_len, rest), F32)], axis=-1)
    per_row = lambda tab: jnp.concatenate([tab[:n] for n in seq_lens], axis=0)
    return per_row(cos_t), per_row(sin_t)


def _trunk(groups, params, tiles):
    (g_pre_mix, w_in, b_gate, w_sink, w_conv, w_attn_out, w_conv_out, w_out,
     g_post_mix, g_pre_mlp, w_mlp_in, w_mlp_out, g_post_mlp) = params
    depth, d, _ = w_in.shape
    seq_lens = [g.shape[1] for g in groups for _ in range(g.shape[0])]
    x = jnp.concatenate([g.reshape(-1, d) for g in groups], axis=0)

    cos_rows, sin_rows = _rope_tables(seq_lens)
    mix_edges = _tile_edges(seq_lens, tiles.mix_rows)

    w_main = _regroup_w_in(w_in)
    w_kv = w_in[:, :, ATTN_WIDTH:ATTN_WIDTH + 2 * KV_WIDTH].astype(BF16)
    w_ao, w_co, w_o = (w.astype(BF16) for w in (w_attn_out, w_conv_out, w_out))
    w1, w2 = w_mlp_in.astype(BF16), w_mlp_out.astype(BF16)
    row = lambda p: p[:, None, :]
    mlp = lambda l, xin, tm, **kw: _mlp(l, xin, row(g_pre_mlp), w1, w2, row(g_post_mlp),
                                        tm, tiles.mlp_cols, **kw)

    for l in range(depth):
        proj, kv = _in_proj(l, x, row(g_pre_mix), w_main, w_kv, cos_rows, sin_rows, tiles)
        x = _mixer(l, x, proj, kv, w_sink, row(b_gate), w_conv, w_ao, w_co, w_o,
                   row(g_post_mix), mix_edges, tiles)
        if l < depth - 1:
            x = mlp(l, x, tiles.mlp_rows)

    outs, r0 = [], 0
    for g in groups:
        n = g.shape[0] * g.shape[1]
        outs.append(mlp(depth - 1, x, tiles.out_rows, row0=r0, rows=n).reshape(g.shape))
        r0 += n
    return tuple(outs)


def kernel(x_prompt, x_sample, g_pre_mix, w_in, b_gate, w_sink, w_conv, w_attn_out, w_conv_out,
           w_out, g_post_mix, g_pre_mlp, w_mlp_in, w_mlp_out, g_post_mlp):
    params = (g_pre_mix, w_in, b_gate, w_sink, w_conv, w_attn_out, w_conv_out, w_out,
              g_post_mix, g_pre_mlp, w_mlp_in, w_mlp_out, g_post_mlp)
    return _trunk([x_prompt, x_sample], params, V7X_TILES)
```

```python
import functools
from typing import NamedTuple

import numpy as np
import jax
import jax.numpy as jnp
from jax import lax
from jax.experimental import pallas as pl
from jax.experimental.pallas import tpu as pltpu

F32 = jnp.float32
BF16 = jnp.bfloat16

HEAD_DIM = 128
N_Q_HEADS = 8
N_KV_HEADS = 2
GROUP = N_Q_HEADS // N_KV_HEADS
ATTN_WIDTH = N_Q_HEADS * HEAD_DIM
KV_WIDTH = N_KV_HEADS * HEAD_DIM
BLOCK = 128
ROT_DIM = HEAD_DIM // 4
ROT_HALF = ROT_DIM // 2
ROPE_THETA = 500000.0
RMS_EPS = 1e-6
NEG_INF = -1e30
LOG2_E = 1.4426950408889634
SUBLANES = 8
BF16_SUBLANES = 16
V7X_VMEM_LIMIT_BYTES = 60000 * 1024


class Tiles(NamedTuple):
    proj_rows: int
    proj_cols: int
    mix_rows: int
    mlp_rows: int
    mlp_cols: int
    out_rows: int


V7X_TILES = Tiles(proj_rows=768, proj_cols=2048, mix_rows=512, mlp_rows=768, mlp_cols=1024,
                  out_rows=512)


def _rmsnorm(x, g):
    return x * lax.rsqrt(jnp.mean(x * x, axis=-1, keepdims=True) + RMS_EPS) * g


def _rope_head(xh, cos, sin, lane):
    partner = jnp.where(lane < ROT_HALF,
                        pltpu.roll(xh, HEAD_DIM - ROT_HALF, 1),
                        pltpu.roll(xh, ROT_HALF, 1))
    return jnp.where(lane < ROT_DIM, xh * cos + partner * sin, xh)


def _params(semantics):
    return pltpu.CompilerParams(dimension_semantics=semantics,
                                vmem_limit_bytes=V7X_VMEM_LIMIT_BYTES)


def _in_proj_kernel(x_ref, g_ref, w_ref, wkv_ref, cos_ref, sin_ref, w1_ref, w2_ref,
                    proj_ref, kv_ref, w1b_ref, w2b_ref, h_ref):
    j = pl.program_id(1)
    rows = x_ref.shape[0]
    lane = lax.broadcasted_iota(jnp.int32, (rows, HEAD_DIM), 1)

    w1b_ref[...] = w1_ref[...].astype(BF16)
    w2b_ref[...] = w2_ref[...].astype(BF16)

    @pl.when(j == 0)
    def _():
        h_ref[...] = _rmsnorm(x_ref[...], g_ref[...]).astype(BF16)
        cos = cos_ref[...]
        sin = sin_ref[...]
        kv = jnp.dot(h_ref[...], wkv_ref[...], preferred_element_type=F32)
        for hd in range(N_KV_HEADS):
            sl = slice(hd * HEAD_DIM, (hd + 1) * HEAD_DIM)
            kv_ref[:, sl] = _rope_head(kv[:, sl], cos, sin, lane).astype(BF16)
        kv_ref[:, KV_WIDTH:] = kv[:, KV_WIDTH:].astype(BF16)
        q = jnp.dot(h_ref[...], w_ref[...], preferred_element_type=F32)
        for hd in range(N_Q_HEADS):
            sl = slice(hd * HEAD_DIM, (hd + 1) * HEAD_DIM)
            proj_ref[:, sl] = _rope_head(q[:, sl], cos, sin, lane).astype(BF16)
        if q.shape[1] > ATTN_WIDTH:
            proj_ref[:, ATTN_WIDTH:] = q[:, ATTN_WIDTH:].astype(BF16)

    @pl.when(j != 0)
    def _():
        proj_ref[...] = jnp.dot(h_ref[...], w_ref[...],
                                preferred_element_type=F32).astype(BF16)


def _in_proj(layer, x, g, w_main, w_kv, cos_rows, sin_rows, w_mlp_in, w_mlp_out, tiles):
    t, d = x.shape
    n_main = w_main.shape[2]
    f = w_mlp_in.shape[2]
    tm, tn = tiles.proj_rows, tiles.proj_cols
    assert tn % ATTN_WIDTH == 0 and n_main % tn == 0 and t % tm == 0
    nj = n_main // tn
    steps = (t // tm) * nj
    r1, r2 = d // steps, f // steps
    assert d % steps == 0 and f % steps == 0 and r1 % BF16_SUBLANES == 0
    step = lambda i, j: i * nj + j
    return pl.pallas_call(
        _in_proj_kernel,
        grid=(t // tm, nj),
        in_specs=[
            pl.BlockSpec((tm, d), lambda i, j: (i, 0)),
            pl.BlockSpec((None, 1, d), lambda i, j: (layer, 0, 0)),
            pl.BlockSpec((None, d, tn), lambda i, j: (layer, 0, j)),
            pl.BlockSpec((None, d, 2 * KV_WIDTH), lambda i, j: (layer, 0, 0),
                         pipeline_mode=pl.Buffered(1)),
            pl.BlockSpec((tm, HEAD_DIM), lambda i, j: (i, 0)),
            pl.BlockSpec((tm, HEAD_DIM), lambda i, j: (i, 0)),
            pl.BlockSpec((None, r1, f), lambda i, j: (layer, step(i, j), 0)),
            pl.BlockSpec((None, r2, d), lambda i, j: (layer, step(i, j), 0)),
        ],
        out_specs=[
            pl.BlockSpec((tm, tn), lambda i, j: (i, j)),
            pl.BlockSpec((tm, 2 * KV_WIDTH), lambda i, j: (i, 0)),
            pl.BlockSpec((r1, f), lambda i, j: (step(i, j), 0)),
            pl.BlockSpec((r2, d), lambda i, j: (step(i, j), 0)),
        ],
        scratch_shapes=[pltpu.VMEM((tm, d), BF16)],
        out_shape=[jax.ShapeDtypeStruct((t, n_main), BF16),
                   jax.ShapeDtypeStruct((t, 2 * KV_WIDTH), BF16),
                   jax.ShapeDtypeStruct((d, f), BF16),
                   jax.ShapeDtypeStruct((f, d), BF16)],
        compiler_params=_params(("arbitrary", "arbitrary")),
        name="in_proj",
    )(x, g, w_main, w_kv, cos_rows, sin_rows, w_mlp_in, w_mlp_out)


def _attention_tile(layer, has_prev, has_next, sink_ref, q_ref, k_ref, v_ref,
                    kp_ref, vp_ref, kn_ref, vn_ref, attn_ref):
    rows = q_ref.shape[0]
    nblk = rows // BLOCK
    r_io = lax.broadcasted_iota(jnp.int32, (BLOCK, BLOCK), 0)
    s_io = lax.broadcasted_iota(jnp.int32, (BLOCK, BLOCK), 1)
    tri_prev = s_io >= r_io
    tri_next = s_io <= r_io
    scale = HEAD_DIM ** -0.5 * LOG2_E

    def window(ref, halo_prev, halo_next, b, csl):
        prev = halo_prev[:, csl] if b == 0 else ref[(b - 1) * BLOCK:b * BLOCK, csl]
        nxt = halo_next[:, csl] if b == nblk - 1 else ref[(b + 1) * BLOCK:(b + 2) * BLOCK, csl]
        return jnp.concatenate([prev, ref[b * BLOCK:(b + 1) * BLOCK, csl], nxt], axis=0)

    for b in range(nblk):
        valid_prev = tri_prev if b > 0 else jnp.logical_and(tri_prev, has_prev)
        valid_next = tri_next if b < nblk - 1 else jnp.logical_and(tri_next, has_next)
        rsl = slice(b * BLOCK, (b + 1) * BLOCK)
        for c in range(N_KV_HEADS):
            csl = slice(c * HEAD_DIM, (c + 1) * HEAD_DIM)
            kb = window(k_ref, kp_ref, kn_ref, b, csl)
            vb = window(v_ref, vp_ref, vn_ref, b, csl)
            heads = [c * GROUP + g for g in range(GROUP)]
            qs = jnp.concatenate(
                [q_ref[rsl, h * HEAD_DIM:(h + 1) * HEAD_DIM] for h in heads], axis=0)
            s = lax.dot_general(qs, kb, (((1,), (1,)), ((), ())),
                                preferred_element_type=F32) * scale
            ps, invs = [], []
            for g, h in enumerate(heads):
                sink = sink_ref[layer, h] * LOG2_E
                sg = s[g * BLOCK:(g + 1) * BLOCK]
                sg = jnp.concatenate(
                    [jnp.where(valid_prev, sg[:, :BLOCK], NEG_INF),
                     sg[:, BLOCK:2 * BLOCK],
                     jnp.where(valid_next, sg[:, 2 * BLOCK:], NEG_INF)], axis=1)
                m = jnp.maximum(jnp.max(sg, axis=-1, keepdims=True), sink)
                p = jnp.exp2(sg - m)
                den = jnp.sum(p, axis=-1, keepdims=True) + jnp.exp2(sink - m)
                ps.append(p.astype(BF16))
                invs.append(1.0 / den)
            o = jnp.dot(jnp.concatenate(ps, axis=0), vb, preferred_element_type=F32)
            for g, h in enumerate(heads):
                attn_ref[rsl, h * HEAD_DIM:(h + 1) * HEAD_DIM] = (
                    o[g * BLOCK:(g + 1) * BLOCK] * invs[g]).astype(BF16)


def _short_conv_tile(has_prev, has_next, b_ref, c_ref, xc_ref, cp_ref, xp_ref, cn_ref, xn_ref,
                     wconv_ref):
    rows = c_ref.shape[0]
    u = c_ref[...].astype(F32) * xc_ref[...].astype(F32)
    up = (cp_ref[SUBLANES - 1:SUBLANES, :].astype(F32)
          * xp_ref[SUBLANES - 1:SUBLANES, :].astype(F32))
    un = cn_ref[0:1, :].astype(F32) * xn_ref[0:1, :].astype(F32)
    up = jnp.where(has_prev, up, 0.0)
    un = jnp.where(has_next, un, 0.0)
    r_io = lax.broadcasted_iota(jnp.int32, u.shape, 0)
    u_prev = jnp.where(r_io == 0, up, pltpu.roll(u, 1, 0))
    u_next = jnp.where(r_io == rows - 1, un, pltpu.roll(u, rows - 1, 0))
    conv = u_prev * wconv_ref[0:1, :] + u * wconv_ref[1:2, :] + u_next * wconv_ref[2:3, :]
    return (b_ref[...].astype(F32) * conv).astype(BF16)


def _mixer_kernel(layer, edge_ref, sink_ref, q_ref, k_ref, v_ref, kp_ref, vp_ref, kn_ref, vn_ref,
                  b_ref, c_ref, xc_ref, cp_ref, xp_ref, cn_ref, xn_ref,
                  ga_ref, gc_ref, bgate_ref, wconv_ref, wao_ref, wco_ref, wout_ref,
                  x_ref, gpost_ref, o_ref, attn_ref):
    i = pl.program_id(0)
    d = x_ref.shape[1]
    has_prev = edge_ref[0, i] == 1
    has_next = edge_ref[1, i] == 1

    _attention_tile(layer, has_prev, has_next, sink_ref, q_ref, k_ref, v_ref,
                    kp_ref, vp_ref, kn_ref, vn_ref, attn_ref)
    conv = _short_conv_tile(has_prev, has_next, b_ref, c_ref, xc_ref, cp_ref, xp_ref,
                            cn_ref, xn_ref, wconv_ref)

    o_a = jnp.dot(attn_ref[...], wao_ref[...], preferred_element_type=F32)
    o_c = jnp.dot(conv, wco_ref[...], preferred_element_type=F32)
    merged = (jax.nn.sigmoid(ga_ref[...].astype(F32) + bgate_ref[:, :d]) * o_a
              + jax.nn.sigmoid(gc_ref[...].astype(F32) + bgate_ref[:, d:]) * o_c)
    y = jnp.dot(merged.astype(BF16), wout_ref[...], preferred_element_type=F32)
    o_ref[...] = x_ref[...] + _rmsnorm(y, gpost_ref[...])


def _mixer(layer, x, proj, kv, sink, b_gate, w_conv, w_ao, w_co, w_out, g_post, edges, tiles):
    t, d = x.shape
    cw = w_co.shape[1]
    tm = tiles.mix_rows
    assert t % tm == 0 and tm % BLOCK == 0
    assert ATTN_WIDTH == cw and (ATTN_WIDTH + 3 * cw) % d == 0
    ga_blk = (ATTN_WIDTH + 3 * cw) // d
    per8, last8 = tm // SUBLANES, t // SUBLANES - 1
    perb, lastb = tm // BLOCK, t // BLOCK - 1
    row_before = lambda per, col: (lambda i, e: (jnp.maximum(i * per - 1, 0), col))
    row_after = lambda per, last, col: (lambda i, e: (jnp.minimum((i + 1) * per, last), col))
    resident = pl.Buffered(1)
    grid_spec = pltpu.PrefetchScalarGridSpec(
        num_scalar_prefetch=1,
        grid=(t // tm,),
        in_specs=[
            pl.BlockSpec(memory_space=pltpu.SMEM),
            pl.BlockSpec((tm, ATTN_WIDTH), lambda i, e: (i, 0)),
            pl.BlockSpec((tm, KV_WIDTH), lambda i, e: (i, 0)),
            pl.BlockSpec((tm, KV_WIDTH), lambda i, e: (i, 1)),
            pl.BlockSpec((BLOCK, KV_WIDTH), row_before(perb, 0)),
            pl.BlockSpec((BLOCK, KV_WIDTH), row_before(perb, 1)),
            pl.BlockSpec((BLOCK, KV_WIDTH), row_after(perb, lastb, 0)),
            pl.BlockSpec((BLOCK, KV_WIDTH), row_after(perb, lastb, 1)),
            pl.BlockSpec((tm, cw), lambda i, e: (i, 1)),
            pl.BlockSpec((tm, cw), lambda i, e: (i, 2)),
            pl.BlockSpec((tm, cw), lambda i, e: (i, 3)),
            pl.BlockSpec((SUBLANES, cw), row_before(per8, 2)),
            pl.BlockSpec((SUBLANES, cw), row_before(per8, 3)),
            pl.BlockSpec((SUBLANES, cw), row_after(per8, last8, 2)),
            pl.BlockSpec((SUBLANES, cw), row_after(per8, last8, 3)),
            pl.BlockSpec((tm, d), lambda i, e: (i, ga_blk)),
            pl.BlockSpec((tm, d), lambda i, e: (i, ga_blk + 1)),
            pl.BlockSpec((None, 1, 2 * d), lambda i, e: (layer, 0, 0)),
            pl.BlockSpec((None, 3, cw), lambda i, e: (layer, 0, 0)),
            pl.BlockSpec((None, ATTN_WIDTH, d), lambda i, e: (layer, 0, 0), pipeline_mode=resident),
            pl.BlockSpec((None, cw, d), lambda i, e: (layer, 0, 0), pipeline_mode=resident),
            pl.BlockSpec((None, d, d), lambda i, e: (layer, 0, 0), pipeline_mode=resident),
            pl.BlockSpec((tm, d), lambda i, e: (i, 0)),
            pl.BlockSpec((None, 1, d), lambda i, e: (layer, 0, 0)),
        ],
        out_specs=pl.BlockSpec((tm, d), lambda i, e: (i, 0)),
        scratch_shapes=[pltpu.VMEM((tm, ATTN_WIDTH), BF16)],
    )
    return pl.pallas_call(
        functools.partial(_mixer_kernel, layer),
        grid_spec=grid_spec,
        out_shape=jax.ShapeDtypeStruct((t, d), F32),
        compiler_params=_params(("arbitrary",)),
        name="mixer",
    )(edges, sink, proj, kv, kv, kv, kv, kv, kv, proj, proj, proj, proj, proj, proj, proj,
      proj, proj, b_gate, w_conv, w_ao, w_co, w_out, x, g_post)


def _mlp_kernel(x_ref, gpre_ref, w1_ref, w2_ref, gpost_ref, o_ref, h_ref):
    j = pl.program_id(1)
    last = pl.num_programs(1) - 1

    def hidden_chunk():
        a = jnp.dot(h_ref[...], w1_ref[...], preferred_element_type=F32)
        a = jnp.square(jnp.maximum(a, 0.0)).astype(BF16)
        return jnp.dot(a, w2_ref[...], preferred_element_type=F32)

    @pl.when(j == 0)
    def _():
        h_ref[...] = _rmsnorm(x_ref[...], gpre_ref[...]).astype(BF16)
        o_ref[...] = hidden_chunk()

    @pl.when(jnp.logical_and(j > 0, j < last))
    def _():
        o_ref[...] += hidden_chunk()

    @pl.when(j == last)
    def _():
        f = o_ref[...] + hidden_chunk()
        o_ref[...] = x_ref[...] + _rmsnorm(f, gpost_ref[...])


def _mlp(layer, x, g_pre, w1, w2, g_post, tm, tf, row0=0, rows=None):
    d = x.shape[1]
    t = x.shape[0] if rows is None else rows
    f = w1.shape[1]
    assert t % tm == 0 and f % tf == 0 and row0 % tm == 0
    blk0 = row0 // tm
    return pl.pallas_call(
        _mlp_kernel,
        grid=(t // tm, f // tf),
        in_specs=[
            pl.BlockSpec((tm, d), lambda i, j: (blk0 + i, 0)),
            pl.BlockSpec((None, 1, d), lambda i, j: (layer, 0, 0)),
            pl.BlockSpec((d, tf), lambda i, j: (0, j)),
            pl.BlockSpec((tf, d), lambda i, j: (j, 0)),
            pl.BlockSpec((None, 1, d), lambda i, j: (layer, 0, 0)),
        ],
        out_specs=pl.BlockSpec((tm, d), lambda i, j: (i, 0)),
        out_shape=jax.ShapeDtypeStruct((t, d), F32),
        scratch_shapes=[pltpu.VMEM((tm, d), BF16)],
        compiler_params=_params(("arbitrary", "arbitrary")),
        name="mlp",
    )(x, g_pre, w1, w2, g_post)


def _cast_kernel(w_ref, o_ref):
    o_ref[...] = w_ref[...].astype(o_ref.dtype)


def _regroup_w_in(w_in):
    depth, d, n = w_in.shape
    cb = 2 * KV_WIDTH
    assert ATTN_WIDTH % cb == 0 and n % cb == 0
    q_blocks = ATTN_WIDTH // cb
    n_main = n - cb
    return pl.pallas_call(
        _cast_kernel,
        grid=(depth, n_main // cb),
        in_specs=[pl.BlockSpec((None, d, cb),
                               lambda l, j: (l, 0, jnp.where(j < q_blocks, j, j + 1)))],
        out_specs=pl.BlockSpec((None, d, cb), lambda l, j: (l, 0, j)),
        out_shape=jax.ShapeDtypeStruct((depth, d, n_main), BF16),
        compiler_params=_params(("arbitrary", "arbitrary")),
        name="regroup_w_in",
    )(w_in)


def _tile_edges(seq_lens, tile):
    prev, nxt = [], []
    for s in seq_lens:
        assert s % tile == 0
        n = s // tile
        prev += [0] + [1] * (n - 1)
        nxt += [1] * (n - 1) + [0]
    return jnp.asarray(np.array([prev, nxt], dtype=np.int32))


def _rope_tables(seq_lens):
    max_len = max(seq_lens)
    inv_freq = ROPE_THETA ** (-jnp.arange(0, ROT_DIM, 2, dtype=F32) / ROT_DIM)
    ang = jnp.arange(max_len).astype(F32)[:, None] * inv_freq[None, :]
    cos, sin = jnp.cos(ang), jnp.sin(ang)
    rest = HEAD_DIM - ROT_DIM
    cos_t = jnp.concatenate([cos, cos, jnp.ones((max_len, rest), F32)], axis=-1)
    sin_t = jnp.concatenate([-sin, sin, jnp.zeros((max_len, rest), F32)], axis=-1)
    per_row = lambda tab: jnp.concatenate([tab[:n] for n in seq_lens], axis=0)
    return per_row(cos_t), per_row(sin_t)


def _trunk(groups, params, tiles):
    (g_pre_mix, w_in, b_gate, w_sink, w_conv, w_attn_out, w_conv_out, w_out,
     g_post_mix, g_pre_mlp, w_mlp_in, w_mlp_out, g_post_mlp) = params
    depth, d, _ = w_in.shape
    seq_lens = [g.shape[1] for g in groups for _ in range(g.shape[0])]
    x = jnp.concatenate([g.reshape(-1, d) for g in groups], axis=0)

    cos_rows, sin_rows = _rope_tables(seq_lens)
    mix_edges = _tile_edges(seq_lens, tiles.mix_rows)

    w_main = _regroup_w_in(w_in)
    w_kv = w_in[:, :, ATTN_WIDTH:ATTN_WIDTH + 2 * KV_WIDTH].astype(BF16)
    w_ao, w_co, w_o = (w.astype(BF16) for w in (w_attn_out, w_conv_out, w_out))
    row = lambda p: p[:, None, :]
    mlp = lambda l, xin, w1, w2, tm, **kw: _mlp(l, xin, row(g_pre_mlp), w1, w2, row(g_post_mlp),
                                                tm, tiles.mlp_cols, **kw)

    for l in range(depth):
        proj, kv, w1, w2 = _in_proj(l, x, row(g_pre_mix), w_main, w_kv, cos_rows, sin_rows,
                                    w_mlp_in, w_mlp_out, tiles)
        x = _mixer(l, x, proj, kv, w_sink, row(b_gate), w_conv, w_ao, w_co, w_o,
                   row(g_post_mix), mix_edges, tiles)
        if l < depth - 1:
            x = mlp(l, x, w1, w2, tiles.mlp_rows)

    outs, r0 = [], 0
    for g in groups:
        n = g.shape[0] * g.shape[1]
        outs.append(mlp(depth - 1, x, w1, w2, tiles.out_rows, row0=r0, rows=n).reshape(g.shape))
        r0 += n
    return tuple(outs)


def kernel(x_prompt, x_sample, g_pre_mix, w_in, b_gate, w_sink, w_conv, w_attn_out, w_conv_out,
           w_out, g_post_mix, g_pre_mlp, w_mlp_in, w_mlp_out, g_post_mlp):
    params = (g_pre_mix, w_in, b_gate, w_sink, w_conv, w_attn_out, w_conv_out, w_out,
              g_post_mix, g_pre_mlp, w_mlp_in, w_mlp_out, g_post_mlp)
    return _trunk([x_prompt, x_sample], params, V7X_TILES)
```

```python
import functools
from typing import NamedTuple

import numpy as np
import jax
import jax.numpy as jnp
from jax import lax
from jax.experimental import pallas as pl
from jax.experimental.pallas import tpu as pltpu

F32 = jnp.float32
BF16 = jnp.bfloat16

HEAD_DIM = 128
N_Q_HEADS = 8
N_KV_HEADS = 2
GROUP = N_Q_HEADS // N_KV_HEADS
ATTN_WIDTH = N_Q_HEADS * HEAD_DIM
KV_WIDTH = N_KV_HEADS * HEAD_DIM
BLOCK = 128
ROT_DIM = HEAD_DIM // 4
ROT_HALF = ROT_DIM // 2
ROPE_THETA = 500000.0
RMS_EPS = 1e-6
NEG_INF = -1e30
LOG2_E = 1.4426950408889634
SUBLANES = 8
BF16_SUBLANES = 16
V7X_VMEM_LIMIT_BYTES = 60000 * 1024


class Tiles(NamedTuple):
    proj_rows: int
    proj_cols: int
    mix_rows: int
    mlp_rows: int
    mlp_cols: int
    out_rows: int


V7X_TILES = Tiles(proj_rows=768, proj_cols=2048, mix_rows=512, mlp_rows=768, mlp_cols=1024,
                  out_rows=512)


def _rmsnorm(x, g):
    return x * lax.rsqrt(jnp.mean(x * x, axis=-1, keepdims=True) + RMS_EPS) * g


def _rope_head(xh, cos, sin, lane):
    partner = jnp.where(lane < ROT_HALF,
                        pltpu.roll(xh, HEAD_DIM - ROT_HALF, 1),
                        pltpu.roll(xh, ROT_HALF, 1))
    return jnp.where(lane < ROT_DIM, xh * cos + partner * sin, xh)


def _params(semantics):
    return pltpu.CompilerParams(dimension_semantics=semantics,
                                vmem_limit_bytes=V7X_VMEM_LIMIT_BYTES)


def _in_proj_kernel(n_plain, regroup_next, x_ref, g_ref, w_ref, wkv_ref, cos_ref, sin_ref, *refs):
    n_src = n_plain + (1 if regroup_next else 0)
    cast_src = refs[:n_src]
    proj_ref, kv_ref = refs[n_src:n_src + 2]
    cast_dst = refs[n_src + 2:-1]
    h_ref = refs[-1]
    j = pl.program_id(1)
    rows = x_ref.shape[0]
    lane = lax.broadcasted_iota(jnp.int32, (rows, HEAD_DIM), 1)

    def cast_slices():
        for src, dst in zip(cast_src[:n_plain], cast_dst[:n_plain]):
            dst[...] = src[...].astype(BF16)
        if regroup_next:
            w = cast_src[n_plain][...]
            main_ref, kv_next_ref = cast_dst[n_plain:]
            kv_end = ATTN_WIDTH + 2 * KV_WIDTH
            main_ref[...] = jnp.concatenate([w[:, :ATTN_WIDTH], w[:, kv_end:]],
                                            axis=1).astype(BF16)
            kv_next_ref[...] = w[:, ATTN_WIDTH:kv_end].astype(BF16)

    @pl.when(j == 0)
    def _():
        h_ref[...] = _rmsnorm(x_ref[...], g_ref[...]).astype(BF16)
        cos = cos_ref[...]
        sin = sin_ref[...]
        kv = jnp.dot(h_ref[...], wkv_ref[...], preferred_element_type=F32)
        for hd in range(N_KV_HEADS):
            sl = slice(hd * HEAD_DIM, (hd + 1) * HEAD_DIM)
            kv_ref[:, sl] = _rope_head(kv[:, sl], cos, sin, lane).astype(BF16)
        kv_ref[:, KV_WIDTH:] = kv[:, KV_WIDTH:].astype(BF16)
        q = jnp.dot(h_ref[...], w_ref[...], preferred_element_type=F32)
        for hd in range(N_Q_HEADS):
            sl = slice(hd * HEAD_DIM, (hd + 1) * HEAD_DIM)
            proj_ref[:, sl] = _rope_head(q[:, sl], cos, sin, lane).astype(BF16)
        if q.shape[1] > ATTN_WIDTH:
            proj_ref[:, ATTN_WIDTH:] = q[:, ATTN_WIDTH:].astype(BF16)
        cast_slices()

    @pl.when(j != 0)
    def _():
        proj_ref[...] = jnp.dot(h_ref[...], w_ref[...],
                                preferred_element_type=F32).astype(BF16)
        cast_slices()


def _in_proj(layer, x, g, w_main, w_kv, cos_rows, sin_rows, cast_plain, w_in_next, tiles):
    t, d = x.shape
    n_main = w_main.shape[1]
    tm, tn = tiles.proj_rows, tiles.proj_cols
    assert tn % ATTN_WIDTH == 0 and n_main % tn == 0 and t % tm == 0
    nj = n_main // tn
    steps = (t // tm) * nj
    step = lambda i, j: i * nj + j

    def slice_specs(w, lyr, out_cols):
        rows, cols = w.shape[1:]
        r = max(BF16_SUBLANES, rows // steps)
        assert rows % r == 0
        last = rows // r - 1
        chunk = lambda i, j: jnp.minimum(step(i, j), last)
        src = pl.BlockSpec((None, r, cols), lambda i, j: (lyr, chunk(i, j), 0))
        dst = [pl.BlockSpec((r, c), lambda i, j: (chunk(i, j), 0)) for c in out_cols]
        shapes = [jax.ShapeDtypeStruct((rows, c), BF16) for c in out_cols]
        return src, dst, shapes

    cast_in, cast_out, cast_shapes, cast_args = [], [], [], []
    for w, lyr in cast_plain:
        src, dst, shapes = slice_specs(w, lyr, [w.shape[2]])
        cast_in.append(src); cast_out += dst; cast_shapes += shapes; cast_args.append(w)
    if w_in_next is not None:
        w, lyr = w_in_next
        kv_cols = 2 * KV_WIDTH
        src, dst, shapes = slice_specs(w, lyr, [w.shape[2] - kv_cols, kv_cols])
        cast_in.append(src); cast_out += dst; cast_shapes += shapes; cast_args.append(w)

    return pl.pallas_call(
        functools.partial(_in_proj_kernel, len(cast_plain), w_in_next is not None),
        grid=(t // tm, nj),
        in_specs=[
            pl.BlockSpec((tm, d), lambda i, j: (i, 0)),
            pl.BlockSpec((None, 1, d), lambda i, j: (layer, 0, 0)),
            pl.BlockSpec((d, tn), lambda i, j: (0, j)),
            pl.BlockSpec((d, 2 * KV_WIDTH), lambda i, j: (0, 0), pipeline_mode=pl.Buffered(1)),
            pl.BlockSpec((tm, HEAD_DIM), lambda i, j: (i, 0)),
            pl.BlockSpec((tm, HEAD_DIM), lambda i, j: (i, 0)),
        ] + cast_in,
        out_specs=[
            pl.BlockSpec((tm, tn), lambda i, j: (i, j)),
            pl.BlockSpec((tm, 2 * KV_WIDTH), lambda i, j: (i, 0)),
        ] + cast_out,
        scratch_shapes=[pltpu.VMEM((tm, d), BF16)],
        out_shape=[jax.ShapeDtypeStruct((t, n_main), BF16),
                   jax.ShapeDtypeStruct((t, 2 * KV_WIDTH), BF16)] + cast_shapes,
        compiler_params=_params(("arbitrary", "arbitrary")),
        name="in_proj",
    )(x, g, w_main, w_kv, cos_rows, sin_rows, *cast_args)


def _attention_tile(layer, has_prev, has_next, sink_ref, q_ref, k_ref, v_ref,
                    kp_ref, vp_ref, kn_ref, vn_ref, attn_ref):
    rows = q_ref.shape[0]
    nblk = rows // BLOCK
    r_io = lax.broadcasted_iota(jnp.int32, (BLOCK, BLOCK), 0)
    s_io = lax.broadcasted_iota(jnp.int32, (BLOCK, BLOCK), 1)
    tri_prev = s_io >= r_io
    tri_next = s_io <= r_io
    scale = HEAD_DIM ** -0.5 * LOG2_E

    def window(ref, halo_prev, halo_next, b, csl):
        prev = halo_prev[:, csl] if b == 0 else ref[(b - 1) * BLOCK:b * BLOCK, csl]
        nxt = halo_next[:, csl] if b == nblk - 1 else ref[(b + 1) * BLOCK:(b + 2) * BLOCK, csl]
        return jnp.concatenate([prev, ref[b * BLOCK:(b + 1) * BLOCK, csl], nxt], axis=0)

    for b in range(nblk):
        valid_prev = tri_prev if b > 0 else jnp.logical_and(tri_prev, has_prev)
        valid_next = tri_next if b < nblk - 1 else jnp.logical_and(tri_next, has_next)
        rsl = slice(b * BLOCK, (b + 1) * BLOCK)
        for c in range(N_KV_HEADS):
            csl = slice(c * HEAD_DIM, (c + 1) * HEAD_DIM)
            kb = window(k_ref, kp_ref, kn_ref, b, csl)
            vb = window(v_ref, vp_ref, vn_ref, b, csl)
            heads = [c * GROUP + g for g in range(GROUP)]
            qs = jnp.concatenate(
                [q_ref[rsl, h * HEAD_DIM:(h + 1) * HEAD_DIM] for h in heads], axis=0)
            s = lax.dot_general(qs, kb, (((1,), (1,)), ((), ())),
                                preferred_element_type=F32) * scale
            ps, invs = [], []
            for g, h in enumerate(heads):
                sink = sink_ref[layer, h] * LOG2_E
                sg = s[g * BLOCK:(g + 1) * BLOCK]
                sg = jnp.concatenate(
                    [jnp.where(valid_prev, sg[:, :BLOCK], NEG_INF),
                     sg[:, BLOCK:2 * BLOCK],
                     jnp.where(valid_next, sg[:, 2 * BLOCK:], NEG_INF)], axis=1)
                m = jnp.maximum(jnp.max(sg, axis=-1, keepdims=True), sink)
                p = jnp.exp2(sg - m)
                den = jnp.sum(p, axis=-1, keepdims=True) + jnp.exp2(sink - m)
                ps.append(p.astype(BF16))
                invs.append(1.0 / den)
            o = jnp.dot(jnp.concatenate(ps, axis=0), vb, preferred_element_type=F32)
            for g, h in enumerate(heads):
                attn_ref[rsl, h * HEAD_DIM:(h + 1) * HEAD_DIM] = (
                    o[g * BLOCK:(g + 1) * BLOCK] * invs[g]).astype(BF16)


def _short_conv_tile(has_prev, has_next, b_ref, c_ref, xc_ref, cp_ref, xp_ref, cn_ref, xn_ref,
                     wconv_ref):
    rows = c_ref.shape[0]
    u = c_ref[...].astype(F32) * xc_ref[...].astype(F32)
    up = (cp_ref[SUBLANES - 1:SUBLANES, :].astype(F32)
          * xp_ref[SUBLANES - 1:SUBLANES, :].astype(F32))
    un = cn_ref[0:1, :].astype(F32) * xn_ref[0:1, :].astype(F32)
    up = jnp.where(has_prev, up, 0.0)
    un = jnp.where(has_next, un, 0.0)
    r_io = lax.broadcasted_iota(jnp.int32, u.shape, 0)
    u_prev = jnp.where(r_io == 0, up, pltpu.roll(u, 1, 0))
    u_next = jnp.where(r_io == rows - 1, un, pltpu.roll(u, rows - 1, 0))
    conv = u_prev * wconv_ref[0:1, :] + u * wconv_ref[1:2, :] + u_next * wconv_ref[2:3, :]
    return (b_ref[...].astype(F32) * conv).astype(BF16)


def _mixer_kernel(layer, edge_ref, sink_ref, q_ref, k_ref, v_ref, kp_ref, vp_ref, kn_ref, vn_ref,
                  b_ref, c_ref, xc_ref, cp_ref, xp_ref, cn_ref, xn_ref,
                  ga_ref, gc_ref, bgate_ref, wconv_ref, wao_ref, wco_ref, wout_ref,
                  x_ref, gpost_ref, o_ref, attn_ref):
    i = pl.program_id(0)
    d = x_ref.shape[1]
    has_prev = edge_ref[0, i] == 1
    has_next = edge_ref[1, i] == 1

    _attention_tile(layer, has_prev, has_next, sink_ref, q_ref, k_ref, v_ref,
                    kp_ref, vp_ref, kn_ref, vn_ref, attn_ref)
    conv = _short_conv_tile(has_prev, has_next, b_ref, c_ref, xc_ref, cp_ref, xp_ref,
                            cn_ref, xn_ref, wconv_ref)

    o_a = jnp.dot(attn_ref[...], wao_ref[...], preferred_element_type=F32)
    o_c = jnp.dot(conv, wco_ref[...], preferred_element_type=F32)
    merged = (jax.nn.sigmoid(ga_ref[...].astype(F32) + bgate_ref[:, :d]) * o_a
              + jax.nn.sigmoid(gc_ref[...].astype(F32) + bgate_ref[:, d:]) * o_c)
    y = jnp.dot(merged.astype(BF16), wout_ref[...], preferred_element_type=F32)
    o_ref[...] = x_ref[...] + _rmsnorm(y, gpost_ref[...])


def _mixer(layer, x, proj, kv, sink, b_gate, w_conv, w_ao, w_co, w_out, g_post, edges, tiles):
    t, d = x.shape
    cw = w_co.shape[0]
    tm = tiles.mix_rows
    assert t % tm == 0 and tm % BLOCK == 0
    assert ATTN_WIDTH == cw and (ATTN_WIDTH + 3 * cw) % d == 0
    ga_blk = (ATTN_WIDTH + 3 * cw) // d
    per8, last8 = tm // SUBLANES, t // SUBLANES - 1
    perb, lastb = tm // BLOCK, t // BLOCK - 1
    row_before = lambda per, col: (lambda i, e: (jnp.maximum(i * per - 1, 0), col))
    row_after = lambda per, last, col: (lambda i, e: (jnp.minimum((i + 1) * per, last), col))
    resident = pl.Buffered(1)
    grid_spec = pltpu.PrefetchScalarGridSpec(
        num_scalar_prefetch=1,
        grid=(t // tm,),
        in_specs=[
            pl.BlockSpec(memory_space=pltpu.SMEM),
            pl.BlockSpec((tm, ATTN_WIDTH), lambda i, e: (i, 0)),
            pl.BlockSpec((tm, KV_WIDTH), lambda i, e: (i, 0)),
            pl.BlockSpec((tm, KV_WIDTH), lambda i, e: (i, 1)),
            pl.BlockSpec((BLOCK, KV_WIDTH), row_before(perb, 0)),
            pl.BlockSpec((BLOCK, KV_WIDTH), row_before(perb, 1)),
            pl.BlockSpec((BLOCK, KV_WIDTH), row_after(perb, lastb, 0)),
            pl.BlockSpec((BLOCK, KV_WIDTH), row_after(perb, lastb, 1)),
            pl.BlockSpec((tm, cw), lambda i, e: (i, 1)),
            pl.BlockSpec((tm, cw), lambda i, e: (i, 2)),
            pl.BlockSpec((tm, cw), lambda i, e: (i, 3)),
            pl.BlockSpec((SUBLANES, cw), row_before(per8, 2)),
            pl.BlockSpec((SUBLANES, cw), row_before(per8, 3)),
            pl.BlockSpec((SUBLANES, cw), row_after(per8, last8, 2)),
            pl.BlockSpec((SUBLANES, cw), row_after(per8, last8, 3)),
            pl.BlockSpec((tm, d), lambda i, e: (i, ga_blk)),
            pl.BlockSpec((tm, d), lambda i, e: (i, ga_blk + 1)),
            pl.BlockSpec((None, 1, 2 * d), lambda i, e: (layer, 0, 0)),
            pl.BlockSpec((None, 3, cw), lambda i, e: (layer, 0, 0)),
            pl.BlockSpec((ATTN_WIDTH, d), lambda i, e: (0, 0), pipeline_mode=resident),
            pl.BlockSpec((cw, d), lambda i, e: (0, 0), pipeline_mode=resident),
            pl.BlockSpec((d, d), lambda i, e: (0, 0), pipeline_mode=resident),
            pl.BlockSpec((tm, d), lambda i, e: (i, 0)),
            pl.BlockSpec((None, 1, d), lambda i, e: (layer, 0, 0)),
        ],
        out_specs=pl.BlockSpec((tm, d), lambda i, e: (i, 0)),
        scratch_shapes=[pltpu.VMEM((tm, ATTN_WIDTH), BF16)],
    )
    return pl.pallas_call(
        functools.partial(_mixer_kernel, layer),
        grid_spec=grid_spec,
        out_shape=jax.ShapeDtypeStruct((t, d), F32),
        compiler_params=_params(("arbitrary",)),
        name="mixer",
    )(edges, sink, proj, kv, kv, kv, kv, kv, kv, proj, proj, proj, proj, proj, proj, proj,
      proj, proj, b_gate, w_conv, w_ao, w_co, w_out, x, g_post)


def _mlp_kernel(x_ref, gpre_ref, w1_ref, w2_ref, gpost_ref, o_ref, h_ref):
    j = pl.program_id(1)
    last = pl.num_programs(1) - 1

    def hidden_chunk():
        a = jnp.dot(h_ref[...], w1_ref[...], preferred_element_type=F32)
        a = jnp.square(jnp.maximum(a, 0.0)).astype(BF16)
        return jnp.dot(a, w2_ref[...], preferred_element_type=F32)

    @pl.when(j == 0)
    def _():
        h_ref[...] = _rmsnorm(x_ref[...], gpre_ref[...]).astype(BF16)
        o_ref[...] = hidden_chunk()

    @pl.when(jnp.logical_and(j > 0, j < last))
    def _():
        o_ref[...] += hidden_chunk()

    @pl.when(j == last)
    def _():
        f = o_ref[...] + hidden_chunk()
        o_ref[...] = x_ref[...] + _rmsnorm(f, gpost_ref[...])


def _mlp(layer, x, g_pre, w1, w2, g_post, tm, tf, row0=0, rows=None):
    d = x.shape[1]
    t = x.shape[0] if rows is None else rows
    f = w1.shape[1]
    assert t % tm == 0 and f % tf == 0 and row0 % tm == 0
    blk0 = row0 // tm
    return pl.pallas_call(
        _mlp_kernel,
        grid=(t // tm, f // tf),
        in_specs=[
            pl.BlockSpec((tm, d), lambda i, j: (blk0 + i, 0)),
            pl.BlockSpec((None, 1, d), lambda i, j: (layer, 0, 0)),
            pl.BlockSpec((d, tf), lambda i, j: (0, j)),
            pl.BlockSpec((tf, d), lambda i, j: (j, 0)),
            pl.BlockSpec((None, 1, d), lambda i, j: (layer, 0, 0)),
        ],
        out_specs=pl.BlockSpec((tm, d), lambda i, j: (i, 0)),
        out_shape=jax.ShapeDtypeStruct((t, d), F32),
        scratch_shapes=[pltpu.VMEM((tm, d), BF16)],
        compiler_params=_params(("arbitrary", "arbitrary")),
        name="mlp",
    )(x, g_pre, w1, w2, g_post)


def _cast_kernel(w_ref, o_ref):
    o_ref[...] = w_ref[...].astype(o_ref.dtype)


def _regroup_w_in(w_in, layer):
    _, d, n = w_in.shape
    cb = 2 * KV_WIDTH
    assert ATTN_WIDTH % cb == 0 and n % cb == 0
    q_blocks = ATTN_WIDTH // cb
    n_main = n - cb
    return pl.pallas_call(
        _cast_kernel,
        grid=(n_main // cb,),
        in_specs=[pl.BlockSpec((None, d, cb),
                               lambda j: (layer, 0, jnp.where(j < q_blocks, j, j + 1)))],
        out_specs=pl.BlockSpec((d, cb), lambda j: (0, j)),
        out_shape=jax.ShapeDtypeStruct((d, n_main), BF16),
        compiler_params=_params(("arbitrary",)),
        name="regroup_w_in",
    )(w_in)


def _tile_edges(seq_lens, tile):
    prev, nxt = [], []
    for s in seq_lens:
        assert s % tile == 0
        n = s // tile
        prev += [0] + [1] * (n - 1)
        nxt += [1] * (n - 1) + [0]
    return jnp.asarray(np.array([prev, nxt], dtype=np.int32))


def _rope_tables(seq_lens):
    max_len = max(seq_lens)
    inv_freq = ROPE_THETA ** (-jnp.arange(0, ROT_DIM, 2, dtype=F32) / ROT_DIM)
    ang = jnp.arange(max_len).astype(F32)[:, None] * inv_freq[None, :]
    cos, sin = jnp.cos(ang), jnp.sin(ang)
    rest = HEAD_DIM - ROT_DIM
    cos_t = jnp.concatenate([cos, cos, jnp.ones((max_len, rest), F32)], axis=-1)
    sin_t = jnp.concatenate([-sin, sin, jnp.zeros((max_len, rest), F32)], axis=-1)
    per_row = lambda tab: jnp.concatenate([tab[:n] for n in seq_lens], axis=0)
    return per_row(cos_t), per_row(sin_t)


def _trunk(groups, params, tiles):
    (g_pre_mix, w_in, b_gate, w_sink, w_conv, w_attn_out, w_conv_out, w_out,
     g_post_mix, g_pre_mlp, w_mlp_in, w_mlp_out, g_post_mlp) = params
    depth, d, _ = w_in.shape
    seq_lens = [g.shape[1] for g in groups for _ in range(g.shape[0])]
    x = jnp.concatenate([g.reshape(-1, d) for g in groups], axis=0)

    cos_rows, sin_rows = _rope_tables(seq_lens)
    mix_edges = _tile_edges(seq_lens, tiles.mix_rows)

    kv_cols = slice(ATTN_WIDTH, ATTN_WIDTH + 2 * KV_WIDTH)
    w_main = _regroup_w_in(w_in, 0)
    w_kv = w_in[0, :, kv_cols].astype(BF16)
    w_ao, w_co, w_o = (w[0].astype(BF16) for w in (w_attn_out, w_conv_out, w_out))
    row = lambda p: p[:, None, :]
    mlp = lambda l, xin, w1, w2, tm, **kw: _mlp(l, xin, row(g_pre_mlp), w1, w2, row(g_post_mlp),
                                                tm, tiles.mlp_cols, **kw)

    for l in range(depth):
        has_next = l + 1 < depth
        cast_plain = [(w_mlp_in, l), (w_mlp_out, l)]
        if has_next:
            cast_plain += [(w_attn_out, l + 1), (w_conv_out, l + 1), (w_out, l + 1)]
        proj, kv, w1, w2, *nxt = _in_proj(
            l, x, row(g_pre_mix), w_main, w_kv, cos_rows, sin_rows, cast_plain,
            (w_in, l + 1) if has_next else None, tiles)
        x = _mixer(l, x, proj, kv, w_sink, row(b_gate), w_conv, w_ao, w_co, w_o,
                   row(g_post_mix), mix_edges, tiles)
        if has_next:
            x = mlp(l, x, w1, w2, tiles.mlp_rows)
            w_ao, w_co, w_o, w_main, w_kv = nxt

    outs, r0 = [], 0
    for g in groups:
        n = g.shape[0] * g.shape[1]
        outs.append(mlp(depth - 1, x, w1, w2, tiles.out_rows, row0=r0, rows=n).reshape(g.shape))
        r0 += n
    return tuple(outs)


def kernel(x_prompt, x_sample, g_pre_mix, w_in, b_gate, w_sink, w_conv, w_attn_out, w_conv_out,
           w_out, g_post_mix, g_pre_mlp, w_mlp_in, w_mlp_out, g_post_mlp):
    params = (g_pre_mix, w_in, b_gate, w_sink, w_conv, w_attn_out, w_conv_out, w_out,
              g_post_mix, g_pre_mlp, w_mlp_in, w_mlp_out, g_post_mlp)
    return _trunk([x_prompt, x_sample], params, V7X_TILES)
```

```python
import functools
from typing import NamedTuple

import numpy as np
import jax
import jax.numpy as jnp
from jax import lax
from jax.experimental import pallas as pl
from jax.experimental.pallas import tpu as pltpu

F32 = jnp.float32
BF16 = jnp.bfloat16

HEAD_DIM = 128
N_Q_HEADS = 8
N_KV_HEADS = 2
GROUP = N_Q_HEADS // N_KV_HEADS
ATTN_WIDTH = N_Q_HEADS * HEAD_DIM
KV_WIDTH = N_KV_HEADS * HEAD_DIM
BLOCK = 128
ROT_DIM = HEAD_DIM // 4
ROT_HALF = ROT_DIM // 2
ROPE_THETA = 500000.0
RMS_EPS = 1e-6
NEG_INF = -1e30
LOG2_E = 1.4426950408889634
SUBLANES = 8
BF16_SUBLANES = 16
V7X_VMEM_LIMIT_BYTES = 60000 * 1024


class Tiles(NamedTuple):
    proj_rows: int
    proj_cols: int
    mix_rows: int
    mlp_rows: int
    mlp_cols: int
    out_rows: int


V7X_TILES = Tiles(proj_rows=768, proj_cols=2048, mix_rows=512, mlp_rows=768, mlp_cols=1024,
                  out_rows=512)


def _rmsnorm(x, g):
    return x * lax.rsqrt(jnp.mean(x * x, axis=-1, keepdims=True) + RMS_EPS) * g


def _rope_head(xh, cos, sin, lane):
    partner = jnp.where(lane < ROT_HALF,
                        pltpu.roll(xh, HEAD_DIM - ROT_HALF, 1),
                        pltpu.roll(xh, ROT_HALF, 1))
    return jnp.where(lane < ROT_DIM, xh * cos + partner * sin, xh)


def _params(semantics):
    return pltpu.CompilerParams(dimension_semantics=semantics,
                                vmem_limit_bytes=V7X_VMEM_LIMIT_BYTES)


def _in_proj_kernel(n_plain, regroup_next, x_ref, g_ref, w_ref, wkv_ref, cos_ref, sin_ref, *refs):
    n_src = n_plain + (1 if regroup_next else 0)
    cast_src = refs[:n_src]
    proj_ref, kv_ref = refs[n_src:n_src + 2]
    cast_dst = refs[n_src + 2:-1]
    h_ref = refs[-1]
    j = pl.program_id(1)
    rows = x_ref.shape[0]
    lane = lax.broadcasted_iota(jnp.int32, (rows, HEAD_DIM), 1)

    def cast_slices():
        for src, dst in zip(cast_src[:n_plain], cast_dst[:n_plain]):
            dst[...] = src[...].astype(BF16)
        if regroup_next:
            w = cast_src[n_plain][...]
            main_ref, kv_next_ref = cast_dst[n_plain:]
            kv_end = ATTN_WIDTH + 2 * KV_WIDTH
            main_ref[...] = jnp.concatenate([w[:, :ATTN_WIDTH], w[:, kv_end:]],
                                            axis=1).astype(BF16)
            kv_next_ref[...] = w[:, ATTN_WIDTH:kv_end].astype(BF16)

    @pl.when(j == 0)
    def _():
        h_ref[...] = _rmsnorm(x_ref[...], g_ref[...]).astype(BF16)
        cos = cos_ref[...]
        sin = sin_ref[...]
        kv = jnp.dot(h_ref[...], wkv_ref[...], preferred_element_type=F32)
        for hd in range(N_KV_HEADS):
            sl = slice(hd * HEAD_DIM, (hd + 1) * HEAD_DIM)
            kv_ref[:, sl] = _rope_head(kv[:, sl], cos, sin, lane).astype(BF16)
        kv_ref[:, KV_WIDTH:] = kv[:, KV_WIDTH:].astype(BF16)
        q = jnp.dot(h_ref[...], w_ref[...], preferred_element_type=F32)
        for hd in range(N_Q_HEADS):
            sl = slice(hd * HEAD_DIM, (hd + 1) * HEAD_DIM)
            proj_ref[:, sl] = _rope_head(q[:, sl], cos, sin, lane).astype(BF16)
        if q.shape[1] > ATTN_WIDTH:
            proj_ref[:, ATTN_WIDTH:] = q[:, ATTN_WIDTH:].astype(BF16)
        cast_slices()

    @pl.when(j != 0)
    def _():
        proj_ref[...] = jnp.dot(h_ref[...], w_ref[...],
                                preferred_element_type=F32).astype(BF16)
        cast_slices()


def _in_proj(layer, x, g, w_main, w_kv, cos_rows, sin_rows, cast_plain, w_in_next, tiles):
    t, d = x.shape
    n_main = w_main.shape[1]
    tm, tn = tiles.proj_rows, tiles.proj_cols
    assert tn % ATTN_WIDTH == 0 and n_main % tn == 0 and t % tm == 0
    nj = n_main // tn
    steps = (t // tm) * nj
    step = lambda i, j: i * nj + j

    def slice_specs(w, lyr, out_cols):
        rows, cols = w.shape[1:]
        r = max(BF16_SUBLANES, rows // steps)
        assert rows % r == 0
        last = rows // r - 1
        chunk = lambda i, j: jnp.minimum(step(i, j), last)
        src = pl.BlockSpec((None, r, cols), lambda i, j: (lyr, chunk(i, j), 0))
        dst = [pl.BlockSpec((r, c), lambda i, j: (chunk(i, j), 0)) for c in out_cols]
        shapes = [jax.ShapeDtypeStruct((rows, c), BF16) for c in out_cols]
        return src, dst, shapes

    cast_in, cast_out, cast_shapes, cast_args = [], [], [], []
    for w, lyr in cast_plain:
        src, dst, shapes = slice_specs(w, lyr, [w.shape[2]])
        cast_in.append(src); cast_out += dst; cast_shapes += shapes; cast_args.append(w)
    if w_in_next is not None:
        w, lyr = w_in_next
        kv_cols = 2 * KV_WIDTH
        src, dst, shapes = slice_specs(w, lyr, [w.shape[2] - kv_cols, kv_cols])
        cast_in.append(src); cast_out += dst; cast_shapes += shapes; cast_args.append(w)

    return pl.pallas_call(
        functools.partial(_in_proj_kernel, len(cast_plain), w_in_next is not None),
        grid=(t // tm, nj),
        in_specs=[
            pl.BlockSpec((tm, d), lambda i, j: (i, 0)),
            pl.BlockSpec((None, 1, d), lambda i, j: (layer, 0, 0)),
            pl.BlockSpec((d, tn), lambda i, j: (0, j)),
            pl.BlockSpec((d, 2 * KV_WIDTH), lambda i, j: (0, 0), pipeline_mode=pl.Buffered(1)),
            pl.BlockSpec((tm, HEAD_DIM), lambda i, j: (i, 0)),
            pl.BlockSpec((tm, HEAD_DIM), lambda i, j: (i, 0)),
        ] + cast_in,
        out_specs=[
            pl.BlockSpec((tm, tn), lambda i, j: (i, j)),
            pl.BlockSpec((tm, 2 * KV_WIDTH), lambda i, j: (i, 0)),
        ] + cast_out,
        scratch_shapes=[pltpu.VMEM((tm, d), BF16)],
        out_shape=[jax.ShapeDtypeStruct((t, n_main), BF16),
                   jax.ShapeDtypeStruct((t, 2 * KV_WIDTH), BF16)] + cast_shapes,
        compiler_params=_params(("arbitrary", "arbitrary")),
        name="in_proj",
    )(x, g, w_main, w_kv, cos_rows, sin_rows, *cast_args)


def _attention_tile(layer, has_prev, has_next, sink_ref, q_ref, k_ref, v_ref,
                    kp_ref, vp_ref, kn_ref, vn_ref, attn_ref):
    rows = q_ref.shape[0]
    nblk = rows // BLOCK
    r_io = lax.broadcasted_iota(jnp.int32, (BLOCK, BLOCK), 0)
    s_io = lax.broadcasted_iota(jnp.int32, (BLOCK, BLOCK), 1)
    tri_prev = s_io >= r_io
    tri_next = s_io <= r_io
    scale = HEAD_DIM ** -0.5 * LOG2_E

    def window(ref, halo_prev, halo_next, b, csl):
        prev = halo_prev[:, csl] if b == 0 else ref[(b - 1) * BLOCK:b * BLOCK, csl]
        nxt = halo_next[:, csl] if b == nblk - 1 else ref[(b + 1) * BLOCK:(b + 2) * BLOCK, csl]
        return jnp.concatenate([prev, ref[b * BLOCK:(b + 1) * BLOCK, csl], nxt], axis=0)

    for b in range(nblk):
        valid_prev = tri_prev if b > 0 else jnp.logical_and(tri_prev, has_prev)
        valid_next = tri_next if b < nblk - 1 else jnp.logical_and(tri_next, has_next)
        rsl = slice(b * BLOCK, (b + 1) * BLOCK)
        for c in range(N_KV_HEADS):
            csl = slice(c * HEAD_DIM, (c + 1) * HEAD_DIM)
            kb = window(k_ref, kp_ref, kn_ref, b, csl)
            vb = window(v_ref, vp_ref, vn_ref, b, csl)
            heads = [c * GROUP + g for g in range(GROUP)]
            qs = jnp.concatenate(
                [q_ref[rsl, h * HEAD_DIM:(h + 1) * HEAD_DIM] for h in heads], axis=0)
            s = lax.dot_general(qs, kb, (((1,), (1,)), ((), ())),
                                preferred_element_type=F32) * scale
            ps, invs = [], []
            for g, h in enumerate(heads):
                sink = sink_ref[layer, h] * LOG2_E
                sg = s[g * BLOCK:(g + 1) * BLOCK]
                sg = jnp.concatenate(
                    [jnp.where(valid_prev, sg[:, :BLOCK], NEG_INF),
                     sg[:, BLOCK:2 * BLOCK],
                     jnp.where(valid_next, sg[:, 2 * BLOCK:], NEG_INF)], axis=1)
                m = jnp.maximum(jnp.max(sg, axis=-1, keepdims=True), sink)
                p = jnp.exp2(sg - m)
                den = jnp.sum(p, axis=-1, keepdims=True) + jnp.exp2(sink - m)
                ps.append(p.astype(BF16))
                invs.append(1.0 / den)
            o = jnp.dot(jnp.concatenate(ps, axis=0), vb, preferred_element_type=F32)
            for g, h in enumerate(heads):
                attn_ref[rsl, h * HEAD_DIM:(h + 1) * HEAD_DIM] = (
                    o[g * BLOCK:(g + 1) * BLOCK] * invs[g]).astype(BF16)


def _short_conv_tile(has_prev, has_next, b_ref, c_ref, xc_ref, cp_ref, xp_ref, cn_ref, xn_ref,
                     wconv_ref):
    rows = c_ref.shape[0]
    u = c_ref[...].astype(F32) * xc_ref[...].astype(F32)
    up = (cp_ref[SUBLANES - 1:SUBLANES, :].astype(F32)
          * xp_ref[SUBLANES - 1:SUBLANES, :].astype(F32))
    un = cn_ref[0:1, :].astype(F32) * xn_ref[0:1, :].astype(F32)
    up = jnp.where(has_prev, up, 0.0)
    un = jnp.where(has_next, un, 0.0)
    r_io = lax.broadcasted_iota(jnp.int32, u.shape, 0)
    u_prev = jnp.where(r_io == 0, up, pltpu.roll(u, 1, 0))
    u_next = jnp.where(r_io == rows - 1, un, pltpu.roll(u, rows - 1, 0))
    conv = u_prev * wconv_ref[0:1, :] + u * wconv_ref[1:2, :] + u_next * wconv_ref[2:3, :]
    return (b_ref[...].astype(F32) * conv).astype(BF16)


def _mixer_kernel(layer, edge_ref, sink_ref, q_ref, k_ref, v_ref, kp_ref, vp_ref, kn_ref, vn_ref,
                  b_ref, c_ref, xc_ref, cp_ref, xp_ref, cn_ref, xn_ref,
                  ga_ref, gc_ref, bgate_ref, wconv_ref, wao_ref, wco_ref, wout_ref,
                  x_ref, gpost_ref, o_ref, attn_ref):
    i = pl.program_id(0)
    d = x_ref.shape[1]
    has_prev = edge_ref[0, i] == 1
    has_next = edge_ref[1, i] == 1

    _attention_tile(layer, has_prev, has_next, sink_ref, q_ref, k_ref, v_ref,
                    kp_ref, vp_ref, kn_ref, vn_ref, attn_ref)
    conv = _short_conv_tile(has_prev, has_next, b_ref, c_ref, xc_ref, cp_ref, xp_ref,
                            cn_ref, xn_ref, wconv_ref)

    o_a = jnp.dot(attn_ref[...], wao_ref[...], preferred_element_type=F32)
    o_c = jnp.dot(conv, wco_ref[...], preferred_element_type=F32)
    merged = (jax.nn.sigmoid(ga_ref[...].astype(F32) + bgate_ref[:, :d]) * o_a
              + jax.nn.sigmoid(gc_ref[...].astype(F32) + bgate_ref[:, d:]) * o_c)
    y = jnp.dot(merged.astype(BF16), wout_ref[...], preferred_element_type=F32)
    o_ref[...] = x_ref[...] + _rmsnorm(y, gpost_ref[...])


def _mixer(layer, x, proj, kv, sink, b_gate, w_conv, w_ao, w_co, w_out, g_post, edges, tiles):
    t, d = x.shape
    cw = w_co.shape[0]
    tm = tiles.mix_rows
    assert t % tm == 0 and tm % BLOCK == 0
    assert ATTN_WIDTH == cw and (ATTN_WIDTH + 3 * cw) % d == 0
    ga_blk = (ATTN_WIDTH + 3 * cw) // d
    per8, last8 = tm // SUBLANES, t // SUBLANES - 1
    perb, lastb = tm // BLOCK, t // BLOCK - 1
    row_before = lambda per, col: (lambda i, e: (jnp.maximum(i * per - 1, 0), col))
    row_after = lambda per, last, col: (lambda i, e: (jnp.minimum((i + 1) * per, last), col))
    resident = pl.Buffered(1)
    grid_spec = pltpu.PrefetchScalarGridSpec(
        num_scalar_prefetch=1,
        grid=(t // tm,),
        in_specs=[
            pl.BlockSpec(memory_space=pltpu.SMEM),
            pl.BlockSpec((tm, ATTN_WIDTH), lambda i, e: (i, 0)),
            pl.BlockSpec((tm, KV_WIDTH), lambda i, e: (i, 0)),
            pl.BlockSpec((tm, KV_WIDTH), lambda i, e: (i, 1)),
            pl.BlockSpec((BLOCK, KV_WIDTH), row_before(perb, 0)),
            pl.BlockSpec((BLOCK, KV_WIDTH), row_before(perb, 1)),
            pl.BlockSpec((BLOCK, KV_WIDTH), row_after(perb, lastb, 0)),
            pl.BlockSpec((BLOCK, KV_WIDTH), row_after(perb, lastb, 1)),
            pl.BlockSpec((tm, cw), lambda i, e: (i, 1)),
            pl.BlockSpec((tm, cw), lambda i, e: (i, 2)),
            pl.BlockSpec((tm, cw), lambda i, e: (i, 3)),
            pl.BlockSpec((SUBLANES, cw), row_before(per8, 2)),
            pl.BlockSpec((SUBLANES, cw), row_before(per8, 3)),
            pl.BlockSpec((SUBLANES, cw), row_after(per8, last8, 2)),
            pl.BlockSpec((SUBLANES, cw), row_after(per8, last8, 3)),
            pl.BlockSpec((tm, d), lambda i, e: (i, ga_blk)),
            pl.BlockSpec((tm, d), lambda i, e: (i, ga_blk + 1)),
            pl.BlockSpec((None, 1, 2 * d), lambda i, e: (layer, 0, 0)),
            pl.BlockSpec((None, 3, cw), lambda i, e: (layer, 0, 0)),
            pl.BlockSpec((ATTN_WIDTH, d), lambda i, e: (0, 0), pipeline_mode=resident),
            pl.BlockSpec((cw, d), lambda i, e: (0, 0), pipeline_mode=resident),
            pl.BlockSpec((d, d), lambda i, e: (0, 0), pipeline_mode=resident),
            pl.BlockSpec((tm, d), lambda i, e: (i, 0)),
            pl.BlockSpec((None, 1, d), lambda i, e: (layer, 0, 0)),
        ],
        out_specs=pl.BlockSpec((tm, d), lambda i, e: (i, 0)),
        scratch_shapes=[pltpu.VMEM((tm, ATTN_WIDTH), BF16)],
    )
    return pl.pallas_call(
        functools.partial(_mixer_kernel, layer),
        grid_spec=grid_spec,
        out_shape=jax.ShapeDtypeStruct((t, d), F32),
        compiler_params=_params(("arbitrary",)),
        name="mixer",
    )(edges, sink, proj, kv, kv, kv, kv, kv, kv, proj, proj, proj, proj, proj, proj, proj,
      proj, proj, b_gate, w_conv, w_ao, w_co, w_out, x, g_post)


def _mlp_kernel(x_ref, gpre_ref, w1_ref, w2_ref, gpost_ref, o_ref, h_ref):
    j = pl.program_id(1)
    last = pl.num_programs(1) - 1

    def hidden_chunk(rows=slice(None)):
        a = jnp.dot(h_ref[rows, :], w1_ref[...], preferred_element_type=F32)
        a = jnp.square(jnp.maximum(a, 0.0)).astype(BF16)
        return jnp.dot(a, w2_ref[...], preferred_element_type=F32)

    sub = 256
    n_sub = x_ref.shape[0] // sub

    @pl.when(j == 0)
    def _():
        h_ref[...] = _rmsnorm(x_ref[...], gpre_ref[...]).astype(BF16)
        o_ref[...] = hidden_chunk()

    @pl.when(jnp.logical_and(j > 0, j < last))
    def _():
        def body(k, carry):
            rows = pl.ds(pl.multiple_of(k * sub, sub), sub)
            o_ref[rows, :] += hidden_chunk(rows)
            return carry
        lax.fori_loop(0, n_sub, body, 0)

    @pl.when(j == last)
    def _():
        f = o_ref[...] + hidden_chunk()
        o_ref[...] = x_ref[...] + _rmsnorm(f, gpost_ref[...])


def _mlp(layer, x, g_pre, w1, w2, g_post, tm, tf, row0=0, rows=None):
    d = x.shape[1]
    t = x.shape[0] if rows is None else rows
    f = w1.shape[1]
    assert t % tm == 0 and f % tf == 0 and row0 % tm == 0
    blk0 = row0 // tm
    return pl.pallas_call(
        _mlp_kernel,
        grid=(t // tm, f // tf),
        in_specs=[
            pl.BlockSpec((tm, d), lambda i, j: (blk0 + i, 0)),
            pl.BlockSpec((None, 1, d), lambda i, j: (layer, 0, 0)),
            pl.BlockSpec((d, tf), lambda i, j: (0, j)),
            pl.BlockSpec((tf, d), lambda i, j: (j, 0)),
            pl.BlockSpec((None, 1, d), lambda i, j: (layer, 0, 0)),
        ],
        out_specs=pl.BlockSpec((tm, d), lambda i, j: (i, 0)),
        out_shape=jax.ShapeDtypeStruct((t, d), F32),
        scratch_shapes=[pltpu.VMEM((tm, d), BF16)],
        compiler_params=_params(("arbitrary", "arbitrary")),
        name="mlp",
    )(x, g_pre, w1, w2, g_post)


def _cast_kernel(w_ref, o_ref):
    o_ref[...] = w_ref[...].astype(o_ref.dtype)


def _regroup_w_in(w_in, layer):
    _, d, n = w_in.shape
    cb = 2 * KV_WIDTH
    assert ATTN_WIDTH % cb == 0 and n % cb == 0
    q_blocks = ATTN_WIDTH // cb
    n_main = n - cb
    return pl.pallas_call(
        _cast_kernel,
        grid=(n_main // cb,),
        in_specs=[pl.BlockSpec((None, d, cb),
                               lambda j: (layer, 0, jnp.where(j < q_blocks, j, j + 1)))],
        out_specs=pl.BlockSpec((d, cb), lambda j: (0, j)),
        out_shape=jax.ShapeDtypeStruct((d, n_main), BF16),
        compiler_params=_params(("arbitrary",)),
        name="regroup_w_in",
    )(w_in)


def _tile_edges(seq_lens, tile):
    prev, nxt = [], []
    for s in seq_lens:
        assert s % tile == 0
        n = s // tile
        prev += [0] + [1] * (n - 1)
        nxt += [1] * (n - 1) + [0]
    return jnp.asarray(np.array([prev, nxt], dtype=np.int32))


def _rope_tables(seq_lens):
    max_len = max(seq_lens)
    inv_freq = ROPE_THETA ** (-jnp.arange(0, ROT_DIM, 2, dtype=F32) / ROT_DIM)
    ang = jnp.arange(max_len).astype(F32)[:, None] * inv_freq[None, :]
    cos, sin = jnp.cos(ang), jnp.sin(ang)
    rest = HEAD_DIM - ROT_DIM
    cos_t = jnp.concatenate([cos, cos, jnp.ones((max_len, rest), F32)], axis=-1)
    sin_t = jnp.concatenate([-sin, sin, jnp.zeros((max_len, rest), F32)], axis=-1)
    per_row = lambda tab: jnp.concatenate([tab[:n] for n in seq_lens], axis=0)
    return per_row(cos_t), per_row(sin_t)


def _trunk(groups, params, tiles):
    (g_pre_mix, w_in, b_gate, w_sink, w_conv, w_attn_out, w_conv_out, w_out,
     g_post_mix, g_pre_mlp, w_mlp_in, w_mlp_out, g_post_mlp) = params
    depth, d, _ = w_in.shape
    seq_lens = [g.shape[1] for g in groups for _ in range(g.shape[0])]
    x = jnp.concatenate([g.reshape(-1, d) for g in groups], axis=0)

    cos_rows, sin_rows = _rope_tables(seq_lens)
    mix_edges = _tile_edges(seq_lens, tiles.mix_rows)

    kv_cols = slice(ATTN_WIDTH, ATTN_WIDTH + 2 * KV_WIDTH)
    w_main = _regroup_w_in(w_in, 0)
    w_kv = w_in[0, :, kv_cols].astype(BF16)
    w_ao, w_co, w_o = (w[0].astype(BF16) for w in (w_attn_out, w_conv_out, w_out))
    row = lambda p: p[:, None, :]
    mlp = lambda l, xin, w1, w2, tm, **kw: _mlp(l, xin, row(g_pre_mlp), w1, w2, row(g_post_mlp),
                                                tm, tiles.mlp_cols, **kw)

    for l in range(depth):
        has_next = l + 1 < depth
        cast_plain = [(w_mlp_in, l), (w_mlp_out, l)]
        if has_next:
            cast_plain += [(w_attn_out, l + 1), (w_conv_out, l + 1), (w_out, l + 1)]
        proj, kv, w1, w2, *nxt = _in_proj(
            l, x, row(g_pre_mix), w_main, w_kv, cos_rows, sin_rows, cast_plain,
            (w_in, l + 1) if has_next else None, tiles)
        x = _mixer(l, x, proj, kv, w_sink, row(b_gate), w_conv, w_ao, w_co, w_o,
                   row(g_post_mix), mix_edges, tiles)
        if has_next:
            x = mlp(l, x, w1, w2, tiles.mlp_rows)
            w_ao, w_co, w_o, w_main, w_kv = nxt

    outs, r0 = [], 0
    for g in groups:
        n = g.shape[0] * g.shape[1]
        outs.append(mlp(depth - 1, x, w1, w2, tiles.out_rows, row0=r0, rows=n).reshape(g.shape))
        r0 += n
    return tuple(outs)


def kernel(x_prompt, x_sample, g_pre_mix, w_in, b_gate, w_sink, w_conv, w_attn_out, w_conv_out,
           w_out, g_post_mix, g_pre_mlp, w_mlp_in, w_mlp_out, g_post_mlp):
    params = (g_pre_mix, w_in, b_gate, w_sink, w_conv, w_attn_out, w_conv_out, w_out,
              g_post_mix, g_pre_mlp, w_mlp_in, w_mlp_out, g_post_mlp)
    return _trunk([x_prompt, x_sample], params, V7X_TILES)
```

```python
import functools
from typing import NamedTuple

import numpy as np
import jax
import jax.numpy as jnp
from jax import lax
from jax.experimental import pallas as pl
from jax.experimental.pallas import tpu as pltpu

F32 = jnp.float32
BF16 = jnp.bfloat16

HEAD_DIM = 128
N_Q_HEADS = 8
N_KV_HEADS = 2
GROUP = N_Q_HEADS // N_KV_HEADS
ATTN_WIDTH = N_Q_HEADS * HEAD_DIM
KV_WIDTH = N_KV_HEADS * HEAD_DIM
BLOCK = 128
ROT_DIM = HEAD_DIM // 4
ROT_HALF = ROT_DIM // 2
ROPE_THETA = 500000.0
RMS_EPS = 1e-6
NEG_INF = -1e30
LOG2_E = 1.4426950408889634
QK_SCALE = HEAD_DIM ** -0.5 * LOG2_E
SUBLANES = 8
BF16_SUBLANES = 16
V7X_VMEM_LIMIT_BYTES = 60000 * 1024


class Tiles(NamedTuple):
    proj_rows: int
    proj_cols: int
    mix_rows: int
    mlp_rows: int
    mlp_cols: int
    out_rows: int


V7X_TILES = Tiles(proj_rows=768, proj_cols=2048, mix_rows=512, mlp_rows=768, mlp_cols=1024,
                  out_rows=512)


def _rmsnorm(x, g):
    return x * lax.rsqrt(jnp.mean(x * x, axis=-1, keepdims=True) + RMS_EPS) * g


def _rope_head(xh, cos, sin, lane):
    partner = jnp.where(lane < ROT_HALF,
                        pltpu.roll(xh, HEAD_DIM - ROT_HALF, 1),
                        pltpu.roll(xh, ROT_HALF, 1))
    return jnp.where(lane < ROT_DIM, xh * cos + partner * sin, xh)


def _params(semantics):
    return pltpu.CompilerParams(dimension_semantics=semantics,
                                vmem_limit_bytes=V7X_VMEM_LIMIT_BYTES)


def _in_proj_kernel(n_plain, regroup_next, x_ref, g_ref, w_ref, wkv_ref, cos_ref, sin_ref,
                    bias_ref, *refs):
    n_src = n_plain + (1 if regroup_next else 0)
    cast_src = refs[:n_src]
    proj_ref, kv_ref = refs[n_src:n_src + 2]
    cast_dst = refs[n_src + 2:-1]
    h_ref = refs[-1]
    j = pl.program_id(1)
    rows = x_ref.shape[0]
    lane = lax.broadcasted_iota(jnp.int32, (rows, HEAD_DIM), 1)

    def cast_slices():
        for src, dst in zip(cast_src[:n_plain], cast_dst[:n_plain]):
            dst[...] = src[...].astype(BF16)
        if regroup_next:
            w = cast_src[n_plain][...]
            main_ref, kv_next_ref = cast_dst[n_plain:]
            kv_end = ATTN_WIDTH + 2 * KV_WIDTH
            main_ref[...] = jnp.concatenate([w[:, :ATTN_WIDTH], w[:, kv_end:]],
                                            axis=1).astype(BF16)
            kv_next_ref[...] = w[:, ATTN_WIDTH:kv_end].astype(BF16)

    @pl.when(j == 0)
    def _():
        h_ref[...] = _rmsnorm(x_ref[...], g_ref[...]).astype(BF16)
        cos = cos_ref[...]
        sin = sin_ref[...]
        kv = jnp.dot(h_ref[...], wkv_ref[...], preferred_element_type=F32)
        for hd in range(N_KV_HEADS):
            sl = slice(hd * HEAD_DIM, (hd + 1) * HEAD_DIM)
            kv_ref[:, sl] = _rope_head(kv[:, sl], cos, sin, lane).astype(BF16)
        kv_ref[:, KV_WIDTH:] = kv[:, KV_WIDTH:].astype(BF16)
        q = jnp.dot(h_ref[...], w_ref[...], preferred_element_type=F32)
        for hd in range(N_Q_HEADS):
            sl = slice(hd * HEAD_DIM, (hd + 1) * HEAD_DIM)
            proj_ref[:, sl] = (_rope_head(q[:, sl], cos, sin, lane) * QK_SCALE).astype(BF16)
        if q.shape[1] > ATTN_WIDTH:
            proj_ref[:, ATTN_WIDTH:] = q[:, ATTN_WIDTH:].astype(BF16)
        cast_slices()

    @pl.when(j != 0)
    def _():
        proj_ref[...] = (jnp.dot(h_ref[...], w_ref[...], preferred_element_type=F32)
                         + bias_ref[...]).astype(BF16)
        cast_slices()


def _in_proj(layer, x, g, w_main, w_kv, cos_rows, sin_rows, bias, cast_plain, w_in_next, tiles):
    t, d = x.shape
    n_main = w_main.shape[1]
    tm, tn = tiles.proj_rows, tiles.proj_cols
    assert tn % ATTN_WIDTH == 0 and n_main % tn == 0 and t % tm == 0
    nj = n_main // tn
    steps = (t // tm) * nj
    step = lambda i, j: i * nj + j

    def slice_specs(w, lyr, out_cols):
        rows, cols = w.shape[1:]
        r = max(BF16_SUBLANES, rows // steps)
        assert rows % r == 0
        last = rows // r - 1
        chunk = lambda i, j: jnp.minimum(step(i, j), last)
        src = pl.BlockSpec((None, r, cols), lambda i, j: (lyr, chunk(i, j), 0))
        dst = [pl.BlockSpec((r, c), lambda i, j: (chunk(i, j), 0)) for c in out_cols]
        shapes = [jax.ShapeDtypeStruct((rows, c), BF16) for c in out_cols]
        return src, dst, shapes

    cast_in, cast_out, cast_shapes, cast_args = [], [], [], []
    for w, lyr in cast_plain:
        src, dst, shapes = slice_specs(w, lyr, [w.shape[2]])
        cast_in.append(src); cast_out += dst; cast_shapes += shapes; cast_args.append(w)
    if w_in_next is not None:
        w, lyr = w_in_next
        kv_cols = 2 * KV_WIDTH
        src, dst, shapes = slice_specs(w, lyr, [w.shape[2] - kv_cols, kv_cols])
        cast_in.append(src); cast_out += dst; cast_shapes += shapes; cast_args.append(w)

    return pl.pallas_call(
        functools.partial(_in_proj_kernel, len(cast_plain), w_in_next is not None),
        grid=(t // tm, nj),
        in_specs=[
            pl.BlockSpec((tm, d), lambda i, j: (i, 0)),
            pl.BlockSpec((None, 1, d), lambda i, j: (layer, 0, 0)),
            pl.BlockSpec((d, tn), lambda i, j: (0, j)),
            pl.BlockSpec((d, 2 * KV_WIDTH), lambda i, j: (0, 0), pipeline_mode=pl.Buffered(1)),
            pl.BlockSpec((tm, HEAD_DIM), lambda i, j: (i, 0)),
            pl.BlockSpec((tm, HEAD_DIM), lambda i, j: (i, 0)),
            pl.BlockSpec((None, 1, tn), lambda i, j: (layer, 0, j)),
        ] + cast_in,
        out_specs=[
            pl.BlockSpec((tm, tn), lambda i, j: (i, j)),
            pl.BlockSpec((tm, 2 * KV_WIDTH), lambda i, j: (i, 0)),
        ] + cast_out,
        scratch_shapes=[pltpu.VMEM((tm, d), BF16)],
        out_shape=[jax.ShapeDtypeStruct((t, n_main), BF16),
                   jax.ShapeDtypeStruct((t, 2 * KV_WIDTH), BF16)] + cast_shapes,
        compiler_params=_params(("arbitrary", "arbitrary")),
        name="in_proj",
    )(x, g, w_main, w_kv, cos_rows, sin_rows, bias, *cast_args)


def _attention_tile(layer, has_prev, has_next, sink_ref, q_ref, k_ref, v_ref,
                    kp_ref, vp_ref, kn_ref, vn_ref, attn_ref):
    rows = q_ref.shape[0]
    nblk = rows // BLOCK
    r_io = lax.broadcasted_iota(jnp.int32, (BLOCK, BLOCK), 0)
    s_io = lax.broadcasted_iota(jnp.int32, (BLOCK, BLOCK), 1)
    tri_prev = s_io >= r_io
    tri_next = s_io <= r_io

    def window(ref, halo_prev, halo_next, b, csl):
        prev = halo_prev[:, csl] if b == 0 else ref[(b - 1) * BLOCK:b * BLOCK, csl]
        nxt = halo_next[:, csl] if b == nblk - 1 else ref[(b + 1) * BLOCK:(b + 2) * BLOCK, csl]
        return jnp.concatenate([prev, ref[b * BLOCK:(b + 1) * BLOCK, csl], nxt], axis=0)

    for b in range(nblk):
        valid_prev = tri_prev if b > 0 else jnp.logical_and(tri_prev, has_prev)
        valid_next = tri_next if b < nblk - 1 else jnp.logical_and(tri_next, has_next)
        rsl = slice(b * BLOCK, (b + 1) * BLOCK)
        for c in range(N_KV_HEADS):
            csl = slice(c * HEAD_DIM, (c + 1) * HEAD_DIM)
            kb = window(k_ref, kp_ref, kn_ref, b, csl)
            vb = window(v_ref, vp_ref, vn_ref, b, csl)
            heads = [c * GROUP + g for g in range(GROUP)]
            qs = jnp.concatenate(
                [q_ref[rsl, h * HEAD_DIM:(h + 1) * HEAD_DIM] for h in heads], axis=0)
            s = lax.dot_general(qs, kb, (((1,), (1,)), ((), ())), preferred_element_type=F32)
            ps, invs = [], []
            for g, h in enumerate(heads):
                sink = sink_ref[layer, h] * LOG2_E
                sg = s[g * BLOCK:(g + 1) * BLOCK]
                sg = jnp.concatenate(
                    [jnp.where(valid_prev, sg[:, :BLOCK], NEG_INF),
                     sg[:, BLOCK:2 * BLOCK],
                     jnp.where(valid_next, sg[:, 2 * BLOCK:], NEG_INF)], axis=1)
                m = jnp.maximum(jnp.max(sg, axis=-1, keepdims=True), sink)
                p = jnp.exp2(sg - m)
                den = jnp.sum(p, axis=-1, keepdims=True) + jnp.exp2(sink - m)
                ps.append(p.astype(BF16))
                invs.append(1.0 / den)
            o = jnp.dot(jnp.concatenate(ps, axis=0), vb, preferred_element_type=F32)
            for g, h in enumerate(heads):
                attn_ref[rsl, h * HEAD_DIM:(h + 1) * HEAD_DIM] = (
                    o[g * BLOCK:(g + 1) * BLOCK] * invs[g]).astype(BF16)


def _short_conv_tile(has_prev, has_next, b_ref, c_ref, xc_ref, cp_ref, xp_ref, cn_ref, xn_ref,
                     wconv_ref):
    rows = c_ref.shape[0]
    u = c_ref[...].astype(F32) * xc_ref[...].astype(F32)
    up = (cp_ref[SUBLANES - 1:SUBLANES, :].astype(F32)
          * xp_ref[SUBLANES - 1:SUBLANES, :].astype(F32))
    un = cn_ref[0:1, :].astype(F32) * xn_ref[0:1, :].astype(F32)
    up = jnp.where(has_prev, up, 0.0)
    un = jnp.where(has_next, un, 0.0)
    r_io = lax.broadcasted_iota(jnp.int32, u.shape, 0)
    u_prev = jnp.where(r_io == 0, up, pltpu.roll(u, 1, 0))
    u_next = jnp.where(r_io == rows - 1, un, pltpu.roll(u, rows - 1, 0))
    conv = u_prev * wconv_ref[0:1, :] + u * wconv_ref[1:2, :] + u_next * wconv_ref[2:3, :]
    return (b_ref[...].astype(F32) * conv).astype(BF16)


def _mixer_kernel(layer, edge_ref, sink_ref, q_ref, k_ref, v_ref, kp_ref, vp_ref, kn_ref, vn_ref,
                  b_ref, c_ref, xc_ref, cp_ref, xp_ref, cn_ref, xn_ref,
                  ga_ref, gc_ref, wconv_ref, wao_ref, wco_ref, wout_ref,
                  x_ref, gpost_ref, o_ref, attn_ref):
    i = pl.program_id(0)
    has_prev = edge_ref[0, i] == 1
    has_next = edge_ref[1, i] == 1

    _attention_tile(layer, has_prev, has_next, sink_ref, q_ref, k_ref, v_ref,
                    kp_ref, vp_ref, kn_ref, vn_ref, attn_ref)
    conv = _short_conv_tile(has_prev, has_next, b_ref, c_ref, xc_ref, cp_ref, xp_ref,
                            cn_ref, xn_ref, wconv_ref)

    o_a = jnp.dot(attn_ref[...], wao_ref[...], preferred_element_type=F32)
    o_c = jnp.dot(conv, wco_ref[...], preferred_element_type=F32)
    merged = (jax.nn.sigmoid(ga_ref[...].astype(F32)) * o_a
              + jax.nn.sigmoid(gc_ref[...].astype(F32)) * o_c)
    y = jnp.dot(merged.astype(BF16), wout_ref[...], preferred_element_type=F32)
    o_ref[...] = x_ref[...] + _rmsnorm(y, gpost_ref[...])


def _mixer(layer, x, proj, kv, sink, w_conv, w_ao, w_co, w_out, g_post, edges, tiles):
    t, d = x.shape
    cw = w_co.shape[0]
    tm = tiles.mix_rows
    assert t % tm == 0 and tm % BLOCK == 0
    assert ATTN_WIDTH == cw and (ATTN_WIDTH + 3 * cw) % d == 0
    ga_blk = (ATTN_WIDTH + 3 * cw) // d
    per8, last8 = tm // SUBLANES, t // SUBLANES - 1
    perb, lastb = tm // BLOCK, t // BLOCK - 1
    row_before = lambda per, col: (lambda i, e: (jnp.maximum(i * per - 1, 0), col))
    row_after = lambda per, last, col: (lambda i, e: (jnp.minimum((i + 1) * per, last), col))
    resident = pl.Buffered(1)
    grid_spec = pltpu.PrefetchScalarGridSpec(
        num_scalar_prefetch=1,
        grid=(t // tm,),
        in_specs=[
            pl.BlockSpec(memory_space=pltpu.SMEM),
            pl.BlockSpec((tm, ATTN_WIDTH), lambda i, e: (i, 0)),
            pl.BlockSpec((tm, KV_WIDTH), lambda i, e: (i, 0)),
            pl.BlockSpec((tm, KV_WIDTH), lambda i, e: (i, 1)),
            pl.BlockSpec((BLOCK, KV_WIDTH), row_before(perb, 0)),
            pl.BlockSpec((BLOCK, KV_WIDTH), row_before(perb, 1)),
            pl.BlockSpec((BLOCK, KV_WIDTH), row_after(perb, lastb, 0)),
            pl.BlockSpec((BLOCK, KV_WIDTH), row_after(perb, lastb, 1)),
            pl.BlockSpec((tm, cw), lambda i, e: (i, 1)),
            pl.BlockSpec((tm, cw), lambda i, e: (i, 2)),
            pl.BlockSpec((tm, cw), lambda i, e: (i, 3)),
            pl.BlockSpec((SUBLANES, cw), row_before(per8, 2)),
            pl.BlockSpec((SUBLANES, cw), row_before(per8, 3)),
            pl.BlockSpec((SUBLANES, cw), row_after(per8, last8, 2)),
            pl.BlockSpec((SUBLANES, cw), row_after(per8, last8, 3)),
            pl.BlockSpec((tm, d), lambda i, e: (i, ga_blk)),
            pl.BlockSpec((tm, d), lambda i, e: (i, ga_blk + 1)),
            pl.BlockSpec((None, 3, cw), lambda i, e: (layer, 0, 0)),
            pl.BlockSpec((ATTN_WIDTH, d), lambda i, e: (0, 0), pipeline_mode=resident),
            pl.BlockSpec((cw, d), lambda i, e: (0, 0), pipeline_mode=resident),
            pl.BlockSpec((d, d), lambda i, e: (0, 0), pipeline_mode=resident),
            pl.BlockSpec((tm, d), lambda i, e: (i, 0)),
            pl.BlockSpec((None, 1, d), lambda i, e: (layer, 0, 0)),
        ],
        out_specs=pl.BlockSpec((tm, d), lambda i, e: (i, 0)),
        scratch_shapes=[pltpu.VMEM((tm, ATTN_WIDTH), BF16)],
    )
    return pl.pallas_call(
        functools.partial(_mixer_kernel, layer),
        grid_spec=grid_spec,
        out_shape=jax.ShapeDtypeStruct((t, d), F32),
        compiler_params=_params(("arbitrary",)),
        name="mixer",
    )(edges, sink, proj, kv, kv, kv, kv, kv, kv, proj, proj, proj, proj, proj, proj, proj,
      proj, proj, w_conv, w_ao, w_co, w_out, x, g_post)


def _mlp_kernel(x_ref, gpre_ref, w1_ref, w2_ref, gpost_ref, o_ref, h_ref):
    j = pl.program_id(1)
    last = pl.num_programs(1) - 1

    def hidden_chunk():
        a = jnp.dot(h_ref[...], w1_ref[...], preferred_element_type=F32)
        a = jnp.square(jnp.maximum(a, 0.0)).astype(BF16)
        return jnp.dot(a, w2_ref[...], preferred_element_type=F32)

    @pl.when(j == 0)
    def _():
        h_ref[...] = _rmsnorm(x_ref[...], gpre_ref[...]).astype(BF16)
        o_ref[...] = hidden_chunk()

    @pl.when(jnp.logical_and(j > 0, j < last))
    def _():
        o_ref[...] += hidden_chunk()

    @pl.when(j == last)
    def _():
        f = o_ref[...] + hidden_chunk()
        o_ref[...] = x_ref[...] + _rmsnorm(f, gpost_ref[...])


def _mlp(layer, x, g_pre, w1, w2, g_post, tm, tf, row0=0, rows=None):
    d = x.shape[1]
    t = x.shape[0] if rows is None else rows
    f = w1.shape[1]
    assert t % tm == 0 and f % tf == 0 and row0 % tm == 0
    blk0 = row0 // tm
    return pl.pallas_call(
        _mlp_kernel,
        grid=(t // tm, f // tf),
        in_specs=[
            pl.BlockSpec((tm, d), lambda i, j: (blk0 + i, 0)),
            pl.BlockSpec((None, 1, d), lambda i, j: (layer, 0, 0)),
            pl.BlockSpec((d, tf), lambda i, j: (0, j)),
            pl.BlockSpec((tf, d), lambda i, j: (j, 0)),
            pl.BlockSpec((None, 1, d), lambda i, j: (layer, 0, 0)),
        ],
        out_specs=pl.BlockSpec((tm, d), lambda i, j: (i, 0)),
        out_shape=jax.ShapeDtypeStruct((t, d), F32),
        scratch_shapes=[pltpu.VMEM((tm, d), BF16)],
        compiler_params=_params(("arbitrary", "arbitrary")),
        name="mlp",
    )(x, g_pre, w1, w2, g_post)


def _cast_kernel(w_ref, o_ref):
    o_ref[...] = w_ref[...].astype(o_ref.dtype)


def _regroup_w_in(w_in, layer):
    _, d, n = w_in.shape
    cb = 2 * KV_WIDTH
    assert ATTN_WIDTH % cb == 0 and n % cb == 0
    q_blocks = ATTN_WIDTH // cb
    n_main = n - cb
    return pl.pallas_call(
        _cast_kernel,
        grid=(n_main // cb,),
        in_specs=[pl.BlockSpec((None, d, cb),
                               lambda j: (layer, 0, jnp.where(j < q_blocks, j, j + 1)))],
        out_specs=pl.BlockSpec((d, cb), lambda j: (0, j)),
        out_shape=jax.ShapeDtypeStruct((d, n_main), BF16),
        compiler_params=_params(("arbitrary",)),
        name="regroup_w_in",
    )(w_in)


def _tile_edges(seq_lens, tile):
    prev, nxt = [], []
    for s in seq_lens:
        assert s % tile == 0
        n = s // tile
        prev += [0] + [1] * (n - 1)
        nxt += [1] * (n - 1) + [0]
    return jnp.asarray(np.array([prev, nxt], dtype=np.int32))


def _rope_tables(seq_lens):
    max_len = max(seq_lens)
    inv_freq = ROPE_THETA ** (-jnp.arange(0, ROT_DIM, 2, dtype=F32) / ROT_DIM)
    ang = jnp.arange(max_len).astype(F32)[:, None] * inv_freq[None, :]
    cos, sin = jnp.cos(ang), jnp.sin(ang)
    rest = HEAD_DIM - ROT_DIM
    cos_t = jnp.concatenate([cos, cos, jnp.ones((max_len, rest), F32)], axis=-1)
    sin_t = jnp.concatenate([-sin, sin, jnp.zeros((max_len, rest), F32)], axis=-1)
    per_row = lambda tab: jnp.concatenate([tab[:n] for n in seq_lens], axis=0)
    return per_row(cos_t), per_row(sin_t)


def _trunk(groups, params, tiles):
    (g_pre_mix, w_in, b_gate, w_sink, w_conv, w_attn_out, w_conv_out, w_out,
     g_post_mix, g_pre_mlp, w_mlp_in, w_mlp_out, g_post_mlp) = params
    depth, d, _ = w_in.shape
    seq_lens = [g.shape[1] for g in groups for _ in range(g.shape[0])]
    x = jnp.concatenate([g.reshape(-1, d) for g in groups], axis=0)

    cos_rows, sin_rows = _rope_tables(seq_lens)
    mix_edges = _tile_edges(seq_lens, tiles.mix_rows)

    kv_cols = slice(ATTN_WIDTH, ATTN_WIDTH + 2 * KV_WIDTH)
    w_main = _regroup_w_in(w_in, 0)
    w_kv = w_in[0, :, kv_cols].astype(BF16)
    w_ao, w_co, w_o = (w[0].astype(BF16) for w in (w_attn_out, w_conv_out, w_out))
    row = lambda p: p[:, None, :]
    n_other = w_in.shape[2] - 2 * KV_WIDTH - b_gate.shape[1]
    bias = row(jnp.concatenate([jnp.zeros((depth, n_other), F32), b_gate], axis=1))
    mlp = lambda l, xin, w1, w2, tm, **kw: _mlp(l, xin, row(g_pre_mlp), w1, w2, row(g_post_mlp),
                                                tm, tiles.mlp_cols, **kw)

    for l in range(depth):
        has_next = l + 1 < depth
        cast_plain = [(w_mlp_in, l), (w_mlp_out, l)]
        if has_next:
            cast_plain += [(w_attn_out, l + 1), (w_conv_out, l + 1), (w_out, l + 1)]
        proj, kv, w1, w2, *nxt = _in_proj(
            l, x, row(g_pre_mix), w_main, w_kv, cos_rows, sin_rows, bias, cast_plain,
            (w_in, l + 1) if has_next else None, tiles)
        x = _mixer(l, x, proj, kv, w_sink, w_conv, w_ao, w_co, w_o,
                   row(g_post_mix), mix_edges, tiles)
        if has_next:
            x = mlp(l, x, w1, w2, tiles.mlp_rows)
            w_ao, w_co, w_o, w_main, w_kv = nxt

    outs, r0 = [], 0
    for g in groups:
        n = g.shape[0] * g.shape[1]
        outs.append(mlp(depth - 1, x, w1, w2, tiles.out_rows, row0=r0, rows=n).reshape(g.shape))
        r0 += n
    return tuple(outs)


def kernel(x_prompt, x_sample, g_pre_mix, w_in, b_gate, w_sink, w_conv, w_attn_out, w_conv_out,
           w_out, g_post_mix, g_pre_mlp, w_mlp_in, w_mlp_out, g_post_mlp):
    params = (g_pre_mix, w_in, b_gate, w_sink, w_conv, w_attn_out, w_conv_out, w_out,
              g_post_mix, g_pre_mlp, w_mlp_in, w_mlp_out, g_post_mlp)
    return _trunk([x_prompt, x_sample], params, V7X_TILES)
```

```python
import functools
from typing import NamedTuple

import numpy as np
import jax
import jax.numpy as jnp
from jax import lax
from jax.experimental import pallas as pl
from jax.experimental.pallas import tpu as pltpu

F32 = jnp.float32
BF16 = jnp.bfloat16

HEAD_DIM = 128
N_Q_HEADS = 8
N_KV_HEADS = 2
GROUP = N_Q_HEADS // N_KV_HEADS
ATTN_WIDTH = N_Q_HEADS * HEAD_DIM
KV_WIDTH = N_KV_HEADS * HEAD_DIM
BLOCK = 128
ROT_DIM = HEAD_DIM // 4
ROT_HALF = ROT_DIM // 2
ROPE_THETA = 500000.0
RMS_EPS = 1e-6
NEG_INF = -1e30
LOG2_E = 1.4426950408889634
QK_SCALE = HEAD_DIM ** -0.5 * LOG2_E
SUBLANES = 8
BF16_SUBLANES = 16
V7X_VMEM_LIMIT_BYTES = 60000 * 1024


class Tiles(NamedTuple):
    proj_rows: int
    proj_cols: int
    mix_rows: int
    mlp_rows: int
    mlp_cols: int
    out_rows: int


V7X_TILES = Tiles(proj_rows=768, proj_cols=2048, mix_rows=512, mlp_rows=768, mlp_cols=1024,
                  out_rows=512)


def _rmsnorm(x, g):
    return x * lax.rsqrt(jnp.mean(x * x, axis=-1, keepdims=True) + RMS_EPS) * g


def _rope_head(xh, cos, sin, lane):
    partner = jnp.where(lane < ROT_HALF,
                        pltpu.roll(xh, HEAD_DIM - ROT_HALF, 1),
                        pltpu.roll(xh, ROT_HALF, 1))
    return jnp.where(lane < ROT_DIM, xh * cos + partner * sin, xh)


def _params(semantics):
    return pltpu.CompilerParams(dimension_semantics=semantics,
                                vmem_limit_bytes=V7X_VMEM_LIMIT_BYTES)


def _in_proj_kernel(n_plain, regroup_next, x_ref, g_ref, w_ref, wkv_ref, cos_ref, sin_ref,
                    bias_ref, colscale_ref, *refs):
    n_src = n_plain + (1 if regroup_next else 0)
    cast_src = refs[:n_src]
    proj_ref, kv_ref = refs[n_src:n_src + 2]
    cast_dst = refs[n_src + 2:-1]
    h_ref = refs[-1]
    j = pl.program_id(1)
    rows = x_ref.shape[0]
    lane = lax.broadcasted_iota(jnp.int32, (rows, HEAD_DIM), 1)

    def cast_slices():
        for src, dst in zip(cast_src[:n_plain], cast_dst[:n_plain]):
            dst[...] = src[...].astype(BF16)
        if regroup_next:
            w = cast_src[n_plain][...]
            main_ref, kv_next_ref = cast_dst[n_plain:]
            kv_end = ATTN_WIDTH + 2 * KV_WIDTH
            main_ref[...] = jnp.concatenate([w[:, :ATTN_WIDTH], w[:, kv_end:]],
                                            axis=1).astype(BF16)
            kv_next_ref[...] = w[:, ATTN_WIDTH:kv_end].astype(BF16)

    @pl.when(j == 0)
    def _():
        h_ref[...] = _rmsnorm(x_ref[...], g_ref[...]).astype(BF16)
        cos = cos_ref[...]
        sin = sin_ref[...]
        kv = jnp.dot(h_ref[...], wkv_ref[...], preferred_element_type=F32)
        for hd in range(N_KV_HEADS):
            sl = slice(hd * HEAD_DIM, (hd + 1) * HEAD_DIM)
            kv_ref[:, sl] = _rope_head(kv[:, sl], cos, sin, lane).astype(BF16)
        kv_ref[:, KV_WIDTH:] = kv[:, KV_WIDTH:].astype(BF16)
        q = jnp.dot(h_ref[...], w_ref[...], preferred_element_type=F32)
        for hd in range(N_Q_HEADS):
            sl = slice(hd * HEAD_DIM, (hd + 1) * HEAD_DIM)
            proj_ref[:, sl] = (_rope_head(q[:, sl], cos, sin, lane) * QK_SCALE).astype(BF16)
        if q.shape[1] > ATTN_WIDTH:
            proj_ref[:, ATTN_WIDTH:] = q[:, ATTN_WIDTH:].astype(BF16)
        cast_slices()

    @pl.when(j != 0)
    def _():
        proj_ref[...] = ((jnp.dot(h_ref[...], w_ref[...], preferred_element_type=F32)
                          + bias_ref[...]) * colscale_ref[...]).astype(BF16)
        cast_slices()


def _in_proj(layer, x, g, w_main, w_kv, cos_rows, sin_rows, bias, colscale, cast_plain, w_in_next,
             tiles):
    t, d = x.shape
    n_main = w_main.shape[1]
    tm, tn = tiles.proj_rows, tiles.proj_cols
    assert tn % ATTN_WIDTH == 0 and n_main % tn == 0 and t % tm == 0
    nj = n_main // tn
    steps = (t // tm) * nj
    step = lambda i, j: i * nj + j

    def slice_specs(w, lyr, out_cols):
        rows, cols = w.shape[1:]
        r = max(BF16_SUBLANES, rows // steps)
        assert rows % r == 0
        last = rows // r - 1
        chunk = lambda i, j: jnp.minimum(step(i, j), last)
        src = pl.BlockSpec((None, r, cols), lambda i, j: (lyr, chunk(i, j), 0))
        dst = [pl.BlockSpec((r, c), lambda i, j: (chunk(i, j), 0)) for c in out_cols]
        shapes = [jax.ShapeDtypeStruct((rows, c), BF16) for c in out_cols]
        return src, dst, shapes

    cast_in, cast_out, cast_shapes, cast_args = [], [], [], []
    for w, lyr in cast_plain:
        src, dst, shapes = slice_specs(w, lyr, [w.shape[2]])
        cast_in.append(src); cast_out += dst; cast_shapes += shapes; cast_args.append(w)
    if w_in_next is not None:
        w, lyr = w_in_next
        kv_cols = 2 * KV_WIDTH
        src, dst, shapes = slice_specs(w, lyr, [w.shape[2] - kv_cols, kv_cols])
        cast_in.append(src); cast_out += dst; cast_shapes += shapes; cast_args.append(w)

    return pl.pallas_call(
        functools.partial(_in_proj_kernel, len(cast_plain), w_in_next is not None),
        grid=(t // tm, nj),
        in_specs=[
            pl.BlockSpec((tm, d), lambda i, j: (i, 0)),
            pl.BlockSpec((None, 1, d), lambda i, j: (layer, 0, 0)),
            pl.BlockSpec((d, tn), lambda i, j: (0, j)),
            pl.BlockSpec((d, 2 * KV_WIDTH), lambda i, j: (0, 0), pipeline_mode=pl.Buffered(1)),
            pl.BlockSpec((tm, HEAD_DIM), lambda i, j: (i, 0)),
            pl.BlockSpec((tm, HEAD_DIM), lambda i, j: (i, 0)),
            pl.BlockSpec((None, 1, tn), lambda i, j: (layer, 0, j)),
            pl.BlockSpec((1, tn), lambda i, j: (0, j)),
        ] + cast_in,
        out_specs=[
            pl.BlockSpec((tm, tn), lambda i, j: (i, j)),
            pl.BlockSpec((tm, 2 * KV_WIDTH), lambda i, j: (i, 0)),
        ] + cast_out,
        scratch_shapes=[pltpu.VMEM((tm, d), BF16)],
        out_shape=[jax.ShapeDtypeStruct((t, n_main), BF16),
                   jax.ShapeDtypeStruct((t, 2 * KV_WIDTH), BF16)] + cast_shapes,
        compiler_params=_params(("arbitrary", "arbitrary")),
        name="in_proj",
    )(x, g, w_main, w_kv, cos_rows, sin_rows, bias, colscale, *cast_args)


def _attention_tile(layer, has_prev, has_next, sink_ref, q_ref, k_ref, v_ref,
                    kp_ref, vp_ref, kn_ref, vn_ref, attn_ref):
    rows = q_ref.shape[0]
    nblk = rows // BLOCK
    r_io = lax.broadcasted_iota(jnp.int32, (BLOCK, BLOCK), 0)
    s_io = lax.broadcasted_iota(jnp.int32, (BLOCK, BLOCK), 1)
    tri_prev = s_io >= r_io
    tri_next = s_io <= r_io

    def window(ref, halo_prev, halo_next, b, csl):
        prev = halo_prev[:, csl] if b == 0 else ref[(b - 1) * BLOCK:b * BLOCK, csl]
        nxt = halo_next[:, csl] if b == nblk - 1 else ref[(b + 1) * BLOCK:(b + 2) * BLOCK, csl]
        return jnp.concatenate([prev, ref[b * BLOCK:(b + 1) * BLOCK, csl], nxt], axis=0)

    for b in range(nblk):
        valid_prev = tri_prev if b > 0 else jnp.logical_and(tri_prev, has_prev)
        valid_next = tri_next if b < nblk - 1 else jnp.logical_and(tri_next, has_next)
        rsl = slice(b * BLOCK, (b + 1) * BLOCK)
        for c in range(N_KV_HEADS):
            csl = slice(c * HEAD_DIM, (c + 1) * HEAD_DIM)
            kb = window(k_ref, kp_ref, kn_ref, b, csl)
            vb = window(v_ref, vp_ref, vn_ref, b, csl)
            heads = [c * GROUP + g for g in range(GROUP)]
            qs = jnp.concatenate(
                [q_ref[rsl, h * HEAD_DIM:(h + 1) * HEAD_DIM] for h in heads], axis=0)
            s = lax.dot_general(qs, kb, (((1,), (1,)), ((), ())), preferred_element_type=F32)
            ps, invs = [], []
            for g, h in enumerate(heads):
                sink = sink_ref[layer, h] * LOG2_E
                sg = s[g * BLOCK:(g + 1) * BLOCK]
                sg = jnp.concatenate(
                    [jnp.where(valid_prev, sg[:, :BLOCK], NEG_INF),
                     sg[:, BLOCK:2 * BLOCK],
                     jnp.where(valid_next, sg[:, 2 * BLOCK:], NEG_INF)], axis=1)
                m = jnp.maximum(jnp.max(sg, axis=-1, keepdims=True), sink)
                p = jnp.exp2(sg - m)
                den = jnp.sum(p, axis=-1, keepdims=True) + jnp.exp2(sink - m)
                ps.append(p.astype(BF16))
                invs.append(1.0 / den)
            o = jnp.dot(jnp.concatenate(ps, axis=0), vb, preferred_element_type=F32)
            for g, h in enumerate(heads):
                attn_ref[rsl, h * HEAD_DIM:(h + 1) * HEAD_DIM] = (
                    o[g * BLOCK:(g + 1) * BLOCK] * invs[g]).astype(BF16)


def _short_conv_tile(has_prev, has_next, b_ref, c_ref, xc_ref, cp_ref, xp_ref, cn_ref, xn_ref,
                     wconv_ref):
    rows = c_ref.shape[0]
    u = c_ref[...].astype(F32) * xc_ref[...].astype(F32)
    up = (cp_ref[SUBLANES - 1:SUBLANES, :].astype(F32)
          * xp_ref[SUBLANES - 1:SUBLANES, :].astype(F32))
    un = cn_ref[0:1, :].astype(F32) * xn_ref[0:1, :].astype(F32)
    up = jnp.where(has_prev, up, 0.0)
    un = jnp.where(has_next, un, 0.0)
    r_io = lax.broadcasted_iota(jnp.int32, u.shape, 0)
    u_prev = jnp.where(r_io == 0, up, pltpu.roll(u, 1, 0))
    u_next = jnp.where(r_io == rows - 1, un, pltpu.roll(u, rows - 1, 0))
    conv = u_prev * wconv_ref[0:1, :] + u * wconv_ref[1:2, :] + u_next * wconv_ref[2:3, :]
    return (b_ref[...].astype(F32) * conv).astype(BF16)


def _mixer_kernel(layer, edge_ref, sink_ref, q_ref, k_ref, v_ref, kp_ref, vp_ref, kn_ref, vn_ref,
                  b_ref, c_ref, xc_ref, cp_ref, xp_ref, cn_ref, xn_ref,
                  ga_ref, gc_ref, wconv_ref, wao_ref, wco_ref, wout_ref,
                  x_ref, gpost_ref, o_ref, attn_ref):
    i = pl.program_id(0)
    has_prev = edge_ref[0, i] == 1
    has_next = edge_ref[1, i] == 1

    _attention_tile(layer, has_prev, has_next, sink_ref, q_ref, k_ref, v_ref,
                    kp_ref, vp_ref, kn_ref, vn_ref, attn_ref)
    conv = _short_conv_tile(has_prev, has_next, b_ref, c_ref, xc_ref, cp_ref, xp_ref,
                            cn_ref, xn_ref, wconv_ref)

    o_a = jnp.dot(attn_ref[...], wao_ref[...], preferred_element_type=F32)
    o_c = jnp.dot(conv, wco_ref[...], preferred_element_type=F32)
    gate = lambda z_ref: 1.0 / (1.0 + jnp.exp2(z_ref[...].astype(F32)))
    merged = gate(ga_ref) * o_a + gate(gc_ref) * o_c
    y = jnp.dot(merged.astype(BF16), wout_ref[...], preferred_element_type=F32)
    o_ref[...] = x_ref[...] + _rmsnorm(y, gpost_ref[...])


def _mixer(layer, x, proj, kv, sink, w_conv, w_ao, w_co, w_out, g_post, edges, tiles):
    t, d = x.shape
    cw = w_co.shape[0]
    tm = tiles.mix_rows
    assert t % tm == 0 and tm % BLOCK == 0
    assert ATTN_WIDTH == cw and (ATTN_WIDTH + 3 * cw) % d == 0
    ga_blk = (ATTN_WIDTH + 3 * cw) // d
    per8, last8 = tm // SUBLANES, t // SUBLANES - 1
    perb, lastb = tm // BLOCK, t // BLOCK - 1
    row_before = lambda per, col: (lambda i, e: (jnp.maximum(i * per - 1, 0), col))
    row_after = lambda per, last, col: (lambda i, e: (jnp.minimum((i + 1) * per, last), col))
    resident = pl.Buffered(1)
    grid_spec = pltpu.PrefetchScalarGridSpec(
        num_scalar_prefetch=1,
        grid=(t // tm,),
        in_specs=[
            pl.BlockSpec(memory_space=pltpu.SMEM),
            pl.BlockSpec((tm, ATTN_WIDTH), lambda i, e: (i, 0)),
            pl.BlockSpec((tm, KV_WIDTH), lambda i, e: (i, 0)),
            pl.BlockSpec((tm, KV_WIDTH), lambda i, e: (i, 1)),
            pl.BlockSpec((BLOCK, KV_WIDTH), row_before(perb, 0)),
            pl.BlockSpec((BLOCK, KV_WIDTH), row_before(perb, 1)),
            pl.BlockSpec((BLOCK, KV_WIDTH), row_after(perb, lastb, 0)),
            pl.BlockSpec((BLOCK, KV_WIDTH), row_after(perb, lastb, 1)),
            pl.BlockSpec((tm, cw), lambda i, e: (i, 1)),
            pl.BlockSpec((tm, cw), lambda i, e: (i, 2)),
            pl.BlockSpec((tm, cw), lambda i, e: (i, 3)),
            pl.BlockSpec((SUBLANES, cw), row_before(per8, 2)),
            pl.BlockSpec((SUBLANES, cw), row_before(per8, 3)),
            pl.BlockSpec((SUBLANES, cw), row_after(per8, last8, 2)),
            pl.BlockSpec((SUBLANES, cw), row_after(per8, last8, 3)),
            pl.BlockSpec((tm, d), lambda i, e: (i, ga_blk)),
            pl.BlockSpec((tm, d), lambda i, e: (i, ga_blk + 1)),
            pl.BlockSpec((None, 3, cw), lambda i, e: (layer, 0, 0)),
            pl.BlockSpec((ATTN_WIDTH, d), lambda i, e: (0, 0), pipeline_mode=resident),
            pl.BlockSpec((cw, d), lambda i, e: (0, 0), pipeline_mode=resident),
            pl.BlockSpec((d, d), lambda i, e: (0, 0), pipeline_mode=resident),
            pl.BlockSpec((tm, d), lambda i, e: (i, 0)),
            pl.BlockSpec((None, 1, d), lambda i, e: (layer, 0, 0)),
        ],
        out_specs=pl.BlockSpec((tm, d), lambda i, e: (i, 0)),
        scratch_shapes=[pltpu.VMEM((tm, ATTN_WIDTH), BF16)],
    )
    return pl.pallas_call(
        functools.partial(_mixer_kernel, layer),
        grid_spec=grid_spec,
        out_shape=jax.ShapeDtypeStruct((t, d), F32),
        compiler_params=_params(("arbitrary",)),
        name="mixer",
    )(edges, sink, proj, kv, kv, kv, kv, kv, kv, proj, proj, proj, proj, proj, proj, proj,
      proj, proj, w_conv, w_ao, w_co, w_out, x, g_post)


def _mlp_kernel(x_ref, gpre_ref, w1_ref, w2_ref, gpost_ref, o_ref, h_ref):
    j = pl.program_id(1)
    last = pl.num_programs(1) - 1

    def hidden_chunk():
        a = jnp.dot(h_ref[...], w1_ref[...], preferred_element_type=F32)
        a = jnp.square(jnp.maximum(a, 0.0)).astype(BF16)
        return jnp.dot(a, w2_ref[...], preferred_element_type=F32)

    @pl.when(j == 0)
    def _():
        h_ref[...] = _rmsnorm(x_ref[...], gpre_ref[...]).astype(BF16)
        o_ref[...] = hidden_chunk()

    @pl.when(jnp.logical_and(j > 0, j < last))
    def _():
        o_ref[...] += hidden_chunk()

    @pl.when(j == last)
    def _():
        f = o_ref[...] + hidden_chunk()
        o_ref[...] = x_ref[...] + _rmsnorm(f, gpost_ref[...])


def _mlp(layer, x, g_pre, w1, w2, g_post, tm, tf, row0=0, rows=None):
    d = x.shape[1]
    t = x.shape[0] if rows is None else rows
    f = w1.shape[1]
    assert t % tm == 0 and f % tf == 0 and row0 % tm == 0
    blk0 = row0 // tm
    return pl.pallas_call(
        _mlp_kernel,
        grid=(t // tm, f // tf),
        in_specs=[
            pl.BlockSpec((tm, d), lambda i, j: (blk0 + i, 0)),
            pl.BlockSpec((None, 1, d), lambda i, j: (layer, 0, 0)),
            pl.BlockSpec((d, tf), lambda i, j: (0, j)),
            pl.BlockSpec((tf, d), lambda i, j: (j, 0)),
            pl.BlockSpec((None, 1, d), lambda i, j: (layer, 0, 0)),
        ],
        out_specs=pl.BlockSpec((tm, d), lambda i, j: (i, 0)),
        out_shape=jax.ShapeDtypeStruct((t, d), F32),
        scratch_shapes=[pltpu.VMEM((tm, d), BF16)],
        compiler_params=_params(("arbitrary", "arbitrary")),
        name="mlp",
    )(x, g_pre, w1, w2, g_post)


def _cast_kernel(w_ref, o_ref):
    o_ref[...] = w_ref[...].astype(o_ref.dtype)


def _regroup_w_in(w_in, layer):
    _, d, n = w_in.shape
    cb = 2 * KV_WIDTH
    assert ATTN_WIDTH % cb == 0 and n % cb == 0
    q_blocks = ATTN_WIDTH // cb
    n_main = n - cb
    return pl.pallas_call(
        _cast_kernel,
        grid=(n_main // cb,),
        in_specs=[pl.BlockSpec((None, d, cb),
                               lambda j: (layer, 0, jnp.where(j < q_blocks, j, j + 1)))],
        out_specs=pl.BlockSpec((d, cb), lambda j: (0, j)),
        out_shape=jax.ShapeDtypeStruct((d, n_main), BF16),
        compiler_params=_params(("arbitrary",)),
        name="regroup_w_in",
    )(w_in)


def _tile_edges(seq_lens, tile):
    prev, nxt = [], []
    for s in seq_lens:
        assert s % tile == 0
        n = s // tile
        prev += [0] + [1] * (n - 1)
        nxt += [1] * (n - 1) + [0]
    return jnp.asarray(np.array([prev, nxt], dtype=np.int32))


def _rope_tables(seq_lens):
    max_len = max(seq_lens)
    inv_freq = ROPE_THETA ** (-jnp.arange(0, ROT_DIM, 2, dtype=F32) / ROT_DIM)
    ang = jnp.arange(max_len).astype(F32)[:, None] * inv_freq[None, :]
    cos, sin = jnp.cos(ang), jnp.sin(ang)
    rest = HEAD_DIM - ROT_DIM
    cos_t = jnp.concatenate([cos, cos, jnp.ones((max_len, rest), F32)], axis=-1)
    sin_t = jnp.concatenate([-sin, sin, jnp.zeros((max_len, rest), F32)], axis=-1)
    per_row = lambda tab: jnp.concatenate([tab[:n] for n in seq_lens], axis=0)
    return per_row(cos_t), per_row(sin_t)


def _trunk(groups, params, tiles):
    (g_pre_mix, w_in, b_gate, w_sink, w_conv, w_attn_out, w_conv_out, w_out,
     g_post_mix, g_pre_mlp, w_mlp_in, w_mlp_out, g_post_mlp) = params
    depth, d, _ = w_in.shape
    seq_lens = [g.shape[1] for g in groups for _ in range(g.shape[0])]
    x = jnp.concatenate([g.reshape(-1, d) for g in groups], axis=0)

    cos_rows, sin_rows = _rope_tables(seq_lens)
    mix_edges = _tile_edges(seq_lens, tiles.mix_rows)

    kv_cols = slice(ATTN_WIDTH, ATTN_WIDTH + 2 * KV_WIDTH)
    w_main = _regroup_w_in(w_in, 0)
    w_kv = w_in[0, :, kv_cols].astype(BF16)
    w_ao, w_co, w_o = (w[0].astype(BF16) for w in (w_attn_out, w_conv_out, w_out))
    row = lambda p: p[:, None, :]
    n_other = w_in.shape[2] - 2 * KV_WIDTH - b_gate.shape[1]
    bias = row(jnp.concatenate([jnp.zeros((depth, n_other), F32), b_gate], axis=1))
    colscale = jnp.concatenate([jnp.ones((1, n_other), F32),
                                jnp.full((1, b_gate.shape[1]), -LOG2_E, F32)], axis=1)
    mlp = lambda l, xin, w1, w2, tm, **kw: _mlp(l, xin, row(g_pre_mlp), w1, w2, row(g_post_mlp),
                                                tm, tiles.mlp_cols, **kw)

    for l in range(depth):
        has_next = l + 1 < depth
        cast_plain = [(w_mlp_in, l), (w_mlp_out, l)]
        if has_next:
            cast_plain += [(w_attn_out, l + 1), (w_conv_out, l + 1), (w_out, l + 1)]
        proj, kv, w1, w2, *nxt = _in_proj(
            l, x, row(g_pre_mix), w_main, w_kv, cos_rows, sin_rows, bias, colscale, cast_plain,
            (w_in, l + 1) if has_next else None, tiles)
        x = _mixer(l, x, proj, kv, w_sink, w_conv, w_ao, w_co, w_o,
                   row(g_post_mix), mix_edges, tiles)
        if has_next:
            x = mlp(l, x, w1, w2, tiles.mlp_rows)
            w_ao, w_co, w_o, w_main, w_kv = nxt

    outs, r0 = [], 0
    for g in groups:
        n = g.shape[0] * g.shape[1]
        outs.append(mlp(depth - 1, x, w1, w2, tiles.out_rows, row0=r0, rows=n).reshape(g.shape))
        r0 += n
    return tuple(outs)


def kernel(x_prompt, x_sample, g_pre_mix, w_in, b_gate, w_sink, w_conv, w_attn_out, w_conv_out,
           w_out, g_post_mix, g_pre_mlp, w_mlp_in, w_mlp_out, g_post_mlp):
    params = (g_pre_mix, w_in, b_gate, w_sink, w_conv, w_attn_out, w_conv_out, w_out,
              g_post_mix, g_pre_mlp, w_mlp_in, w_mlp_out, g_post_mlp)
    return _trunk([x_prompt, x_sample], params, V7X_TILES)
```

```python
import functools
from typing import NamedTuple

import numpy as np
import jax
import jax.numpy as jnp
from jax import lax
from jax.experimental import pallas as pl
from jax.experimental.pallas import tpu as pltpu

F32 = jnp.float32
BF16 = jnp.bfloat16

HEAD_DIM = 128
N_Q_HEADS = 8
N_KV_HEADS = 2
GROUP = N_Q_HEADS // N_KV_HEADS
ATTN_WIDTH = N_Q_HEADS * HEAD_DIM
KV_WIDTH = N_KV_HEADS * HEAD_DIM
BLOCK = 128
ROT_DIM = HEAD_DIM // 4
ROT_HALF = ROT_DIM // 2
ROPE_THETA = 500000.0
RMS_EPS = 1e-6
NEG_INF = -1e30
LOG2_E = 1.4426950408889634
QK_SCALE = HEAD_DIM ** -0.5 * LOG2_E
SUBLANES = 8
BF16_SUBLANES = 16
V7X_VMEM_LIMIT_BYTES = 60000 * 1024


class Tiles(NamedTuple):
    proj_rows: int
    proj_cols: int
    mix_rows: int
    mlp_rows: int
    mlp_cols: int
    out_rows: int
    out_cols: int


V7X_TILES = Tiles(proj_rows=768, proj_cols=2048, mix_rows=512, mlp_rows=768, mlp_cols=1024,
                  out_rows=512, out_cols=2048)


def _rmsnorm(x, g):
    return x * lax.rsqrt(jnp.mean(x * x, axis=-1, keepdims=True) + RMS_EPS) * g


def _rope_head(xh, cos, sin, lane):
    partner = jnp.where(lane < ROT_HALF,
                        pltpu.roll(xh, HEAD_DIM - ROT_HALF, 1),
                        pltpu.roll(xh, ROT_HALF, 1))
    return jnp.where(lane < ROT_DIM, xh * cos + partner * sin, xh)


def _params(semantics):
    return pltpu.CompilerParams(dimension_semantics=semantics,
                                vmem_limit_bytes=V7X_VMEM_LIMIT_BYTES)


def _in_proj_kernel(n_plain, regroup_next, x_ref, g_ref, w_ref, wkv_ref, cos_ref, sin_ref,
                    bias_ref, *refs):
    n_src = n_plain + (1 if regroup_next else 0)
    cast_src = refs[:n_src]
    proj_ref, kv_ref = refs[n_src:n_src + 2]
    cast_dst = refs[n_src + 2:-1]
    h_ref = refs[-1]
    j = pl.program_id(1)
    rows = x_ref.shape[0]
    lane = lax.broadcasted_iota(jnp.int32, (rows, HEAD_DIM), 1)

    def cast_slices():
        for src, dst in zip(cast_src[:n_plain], cast_dst[:n_plain]):
            dst[...] = src[...].astype(BF16)
        if regroup_next:
            w = cast_src[n_plain][...]
            main_ref, kv_next_ref = cast_dst[n_plain:]
            kv_end = ATTN_WIDTH + 2 * KV_WIDTH
            main_ref[...] = jnp.concatenate([w[:, :ATTN_WIDTH], w[:, kv_end:]],
                                            axis=1).astype(BF16)
            kv_next_ref[...] = w[:, ATTN_WIDTH:kv_end].astype(BF16)

    @pl.when(j == 0)
    def _():
        h_ref[...] = _rmsnorm(x_ref[...], g_ref[...]).astype(BF16)
        cos = cos_ref[...]
        sin = sin_ref[...]
        kv = jnp.dot(h_ref[...], wkv_ref[...], preferred_element_type=F32)
        for hd in range(N_KV_HEADS):
            sl = slice(hd * HEAD_DIM, (hd + 1) * HEAD_DIM)
            kv_ref[:, sl] = _rope_head(kv[:, sl], cos, sin, lane).astype(BF16)
        kv_ref[:, KV_WIDTH:] = kv[:, KV_WIDTH:].astype(BF16)
        q = jnp.dot(h_ref[...], w_ref[...], preferred_element_type=F32)
        for hd in range(N_Q_HEADS):
            sl = slice(hd * HEAD_DIM, (hd + 1) * HEAD_DIM)
            proj_ref[:, sl] = (_rope_head(q[:, sl], cos, sin, lane) * QK_SCALE).astype(BF16)
        if q.shape[1] > ATTN_WIDTH:
            proj_ref[:, ATTN_WIDTH:] = q[:, ATTN_WIDTH:].astype(BF16)
        cast_slices()

    @pl.when(j != 0)
    def _():
        proj_ref[...] = (jnp.dot(h_ref[...], w_ref[...], preferred_element_type=F32)
                         + bias_ref[...]).astype(BF16)
        cast_slices()


def _in_proj(layer, x, g, w_main, w_kv, cos_rows, sin_rows, bias, cast_plain, w_in_next, tiles):
    t, d = x.shape
    n_main = w_main.shape[1]
    tm, tn = tiles.proj_rows, tiles.proj_cols
    assert tn % ATTN_WIDTH == 0 and n_main % tn == 0 and t % tm == 0
    nj = n_main // tn
    steps = (t // tm) * nj
    step = lambda i, j: i * nj + j

    def slice_specs(w, lyr, out_cols):
        rows, cols = w.shape[1:]
        r = max(BF16_SUBLANES, rows // steps)
        assert rows % r == 0
        last = rows // r - 1
        chunk = lambda i, j: jnp.minimum(step(i, j), last)
        src = pl.BlockSpec((None, r, cols), lambda i, j: (lyr, chunk(i, j), 0))
        dst = [pl.BlockSpec((r, c), lambda i, j: (chunk(i, j), 0)) for c in out_cols]
        shapes = [jax.ShapeDtypeStruct((rows, c), BF16) for c in out_cols]
        return src, dst, shapes

    cast_in, cast_out, cast_shapes, cast_args = [], [], [], []
    for w, lyr in cast_plain:
        src, dst, shapes = slice_specs(w, lyr, [w.shape[2]])
        cast_in.append(src); cast_out += dst; cast_shapes += shapes; cast_args.append(w)
    if w_in_next is not None:
        w, lyr = w_in_next
        kv_cols = 2 * KV_WIDTH
        src, dst, shapes = slice_specs(w, lyr, [w.shape[2] - kv_cols, kv_cols])
        cast_in.append(src); cast_out += dst; cast_shapes += shapes; cast_args.append(w)

    return pl.pallas_call(
        functools.partial(_in_proj_kernel, len(cast_plain), w_in_next is not None),
        grid=(t // tm, nj),
        in_specs=[
            pl.BlockSpec((tm, d), lambda i, j: (i, 0)),
            pl.BlockSpec((None, 1, d), lambda i, j: (layer, 0, 0)),
            pl.BlockSpec((d, tn), lambda i, j: (0, j)),
            pl.BlockSpec((d, 2 * KV_WIDTH), lambda i, j: (0, 0), pipeline_mode=pl.Buffered(1)),
            pl.BlockSpec((tm, HEAD_DIM), lambda i, j: (i, 0)),
            pl.BlockSpec((tm, HEAD_DIM), lambda i, j: (i, 0)),
            pl.BlockSpec((None, 1, tn), lambda i, j: (layer, 0, j)),
        ] + cast_in,
        out_specs=[
            pl.BlockSpec((tm, tn), lambda i, j: (i, j)),
            pl.BlockSpec((tm, 2 * KV_WIDTH), lambda i, j: (i, 0)),
        ] + cast_out,
        scratch_shapes=[pltpu.VMEM((tm, d), BF16)],
        out_shape=[jax.ShapeDtypeStruct((t, n_main), BF16),
                   jax.ShapeDtypeStruct((t, 2 * KV_WIDTH), BF16)] + cast_shapes,
        compiler_params=_params(("arbitrary", "arbitrary")),
        name="in_proj",
    )(x, g, w_main, w_kv, cos_rows, sin_rows, bias, *cast_args)


def _attention_tile(layer, has_prev, has_next, sink_ref, q_ref, k_ref, v_ref,
                    kp_ref, vp_ref, kn_ref, vn_ref, attn_ref):
    rows = q_ref.shape[0]
    nblk = rows // BLOCK
    r_io = lax.broadcasted_iota(jnp.int32, (BLOCK, BLOCK), 0)
    s_io = lax.broadcasted_iota(jnp.int32, (BLOCK, BLOCK), 1)
    tri_prev = s_io >= r_io
    tri_next = s_io <= r_io

    def window(ref, halo_prev, halo_next, b, csl):
        prev = halo_prev[:, csl] if b == 0 else ref[(b - 1) * BLOCK:b * BLOCK, csl]
        nxt = halo_next[:, csl] if b == nblk - 1 else ref[(b + 1) * BLOCK:(b + 2) * BLOCK, csl]
        return jnp.concatenate([prev, ref[b * BLOCK:(b + 1) * BLOCK, csl], nxt], axis=0)

    for b in range(nblk):
        valid_prev = tri_prev if b > 0 else jnp.logical_and(tri_prev, has_prev)
        valid_next = tri_next if b < nblk - 1 else jnp.logical_and(tri_next, has_next)
        rsl = slice(b * BLOCK, (b + 1) * BLOCK)
        for c in range(N_KV_HEADS):
            csl = slice(c * HEAD_DIM, (c + 1) * HEAD_DIM)
            kb = window(k_ref, kp_ref, kn_ref, b, csl)
            vb = window(v_ref, vp_ref, vn_ref, b, csl)
            heads = [c * GROUP + g for g in range(GROUP)]
            qs = jnp.concatenate(
                [q_ref[rsl, h * HEAD_DIM:(h + 1) * HEAD_DIM] for h in heads], axis=0)
            s = lax.dot_general(qs, kb, (((1,), (1,)), ((), ())), preferred_element_type=F32)
            ps, invs = [], []
            for g, h in enumerate(heads):
                sink = sink_ref[layer, h] * LOG2_E
                sg = s[g * BLOCK:(g + 1) * BLOCK]
                sg = jnp.concatenate(
                    [jnp.where(valid_prev, sg[:, :BLOCK], NEG_INF),
                     sg[:, BLOCK:2 * BLOCK],
                     jnp.where(valid_next, sg[:, 2 * BLOCK:], NEG_INF)], axis=1)
                m = jnp.maximum(jnp.max(sg, axis=-1, keepdims=True), sink)
                p = jnp.exp2(sg - m)
                den = jnp.sum(p, axis=-1, keepdims=True) + jnp.exp2(sink - m)
                ps.append(p.astype(BF16))
                invs.append(1.0 / den)
            o = jnp.dot(jnp.concatenate(ps, axis=0), vb, preferred_element_type=F32)
            for g, h in enumerate(heads):
                attn_ref[rsl, h * HEAD_DIM:(h + 1) * HEAD_DIM] = (
                    o[g * BLOCK:(g + 1) * BLOCK] * invs[g]).astype(BF16)


def _short_conv_tile(has_prev, has_next, b_ref, c_ref, xc_ref, cp_ref, xp_ref, cn_ref, xn_ref,
                     wconv_ref):
    rows = c_ref.shape[0]
    u = c_ref[...].astype(F32) * xc_ref[...].astype(F32)
    up = (cp_ref[SUBLANES - 1:SUBLANES, :].astype(F32)
          * xp_ref[SUBLANES - 1:SUBLANES, :].astype(F32))
    un = cn_ref[0:1, :].astype(F32) * xn_ref[0:1, :].astype(F32)
    up = jnp.where(has_prev, up, 0.0)
    un = jnp.where(has_next, un, 0.0)
    r_io = lax.broadcasted_iota(jnp.int32, u.shape, 0)
    u_prev = jnp.where(r_io == 0, up, pltpu.roll(u, 1, 0))
    u_next = jnp.where(r_io == rows - 1, un, pltpu.roll(u, rows - 1, 0))
    conv = u_prev * wconv_ref[0:1, :] + u * wconv_ref[1:2, :] + u_next * wconv_ref[2:3, :]
    return (b_ref[...].astype(F32) * conv).astype(BF16)


def _mixer_kernel(layer, edge_ref, sink_ref, q_ref, k_ref, v_ref, kp_ref, vp_ref, kn_ref, vn_ref,
                  b_ref, c_ref, xc_ref, cp_ref, xp_ref, cn_ref, xn_ref,
                  ga_ref, gc_ref, wconv_ref, wao_ref, wco_ref, wout_ref,
                  x_ref, gpost_ref, o_ref, attn_ref):
    i = pl.program_id(0)
    has_prev = edge_ref[0, i] == 1
    has_next = edge_ref[1, i] == 1

    _attention_tile(layer, has_prev, has_next, sink_ref, q_ref, k_ref, v_ref,
                    kp_ref, vp_ref, kn_ref, vn_ref, attn_ref)
    conv = _short_conv_tile(has_prev, has_next, b_ref, c_ref, xc_ref, cp_ref, xp_ref,
                            cn_ref, xn_ref, wconv_ref)

    o_a = jnp.dot(attn_ref[...], wao_ref[...], preferred_element_type=F32)
    o_c = jnp.dot(conv, wco_ref[...], preferred_element_type=F32)
    merged = (jax.nn.sigmoid(ga_ref[...].astype(F32)) * o_a
              + jax.nn.sigmoid(gc_ref[...].astype(F32)) * o_c)
    y = jnp.dot(merged.astype(BF16), wout_ref[...], preferred_element_type=F32)
    o_ref[...] = x_ref[...] + _rmsnorm(y, gpost_ref[...])


def _mixer(layer, x, proj, kv, sink, w_conv, w_ao, w_co, w_out, g_post, edges, tiles):
    t, d = x.shape
    cw = w_co.shape[0]
    tm = tiles.mix_rows
    assert t % tm == 0 and tm % BLOCK == 0
    assert ATTN_WIDTH == cw and (ATTN_WIDTH + 3 * cw) % d == 0
    ga_blk = (ATTN_WIDTH + 3 * cw) // d
    per8, last8 = tm // SUBLANES, t // SUBLANES - 1
    perb, lastb = tm // BLOCK, t // BLOCK - 1
    row_before = lambda per, col: (lambda i, e: (jnp.maximum(i * per - 1, 0), col))
    row_after = lambda per, last, col: (lambda i, e: (jnp.minimum((i + 1) * per, last), col))
    resident = pl.Buffered(1)
    grid_spec = pltpu.PrefetchScalarGridSpec(
        num_scalar_prefetch=1,
        grid=(t // tm,),
        in_specs=[
            pl.BlockSpec(memory_space=pltpu.SMEM),
            pl.BlockSpec((tm, ATTN_WIDTH), lambda i, e: (i, 0)),
            pl.BlockSpec((tm, KV_WIDTH), lambda i, e: (i, 0)),
            pl.BlockSpec((tm, KV_WIDTH), lambda i, e: (i, 1)),
            pl.BlockSpec((BLOCK, KV_WIDTH), row_before(perb, 0)),
            pl.BlockSpec((BLOCK, KV_WIDTH), row_before(perb, 1)),
            pl.BlockSpec((BLOCK, KV_WIDTH), row_after(perb, lastb, 0)),
            pl.BlockSpec((BLOCK, KV_WIDTH), row_after(perb, lastb, 1)),
            pl.BlockSpec((tm, cw), lambda i, e: (i, 1)),
            pl.BlockSpec((tm, cw), lambda i, e: (i, 2)),
            pl.BlockSpec((tm, cw), lambda i, e: (i, 3)),
            pl.BlockSpec((SUBLANES, cw), row_before(per8, 2)),
            pl.BlockSpec((SUBLANES, cw), row_before(per8, 3)),
            pl.BlockSpec((SUBLANES, cw), row_after(per8, last8, 2)),
            pl.BlockSpec((SUBLANES, cw), row_after(per8, last8, 3)),
            pl.BlockSpec((tm, d), lambda i, e: (i, ga_blk)),
            pl.BlockSpec((tm, d), lambda i, e: (i, ga_blk + 1)),
            pl.BlockSpec((None, 3, cw), lambda i, e: (layer, 0, 0)),
            pl.BlockSpec((ATTN_WIDTH, d), lambda i, e: (0, 0), pipeline_mode=resident),
            pl.BlockSpec((cw, d), lambda i, e: (0, 0), pipeline_mode=resident),
            pl.BlockSpec((d, d), lambda i, e: (0, 0), pipeline_mode=resident),
            pl.BlockSpec((tm, d), lambda i, e: (i, 0)),
            pl.BlockSpec((None, 1, d), lambda i, e: (layer, 0, 0)),
        ],
        out_specs=pl.BlockSpec((tm, d), lambda i, e: (i, 0)),
        scratch_shapes=[pltpu.VMEM((tm, ATTN_WIDTH), BF16)],
    )
    return pl.pallas_call(
        functools.partial(_mixer_kernel, layer),
        grid_spec=grid_spec,
        out_shape=jax.ShapeDtypeStruct((t, d), F32),
        compiler_params=_params(("arbitrary",)),
        name="mixer",
    )(edges, sink, proj, kv, kv, kv, kv, kv, kv, proj, proj, proj, proj, proj, proj, proj,
      proj, proj, w_conv, w_ao, w_co, w_out, x, g_post)


def _mlp_kernel(x_ref, gpre_ref, w1_ref, w2_ref, gpost_ref, o_ref, h_ref):
    j = pl.program_id(1)
    last = pl.num_programs(1) - 1

    def hidden_chunk():
        a = jnp.dot(h_ref[...], w1_ref[...], preferred_element_type=F32)
        a = jnp.square(jnp.maximum(a, 0.0)).astype(BF16)
        return jnp.dot(a, w2_ref[...], preferred_element_type=F32)

    @pl.when(j == 0)
    def _():
        h_ref[...] = _rmsnorm(x_ref[...], gpre_ref[...]).astype(BF16)
        o_ref[...] = hidden_chunk()

    @pl.when(jnp.logical_and(j > 0, j < last))
    def _():
        o_ref[...] += hidden_chunk()

    @pl.when(j == last)
    def _():
        f = o_ref[...] + hidden_chunk()
        o_ref[...] = x_ref[...] + _rmsnorm(f, gpost_ref[...])


def _mlp(layer, x, g_pre, w1, w2, g_post, tm, tf, row0=0, rows=None):
    d = x.shape[1]
    t = x.shape[0] if rows is None else rows
    f = w1.shape[1]
    assert t % tm == 0 and f % tf == 0 and row0 % tm == 0
    blk0 = row0 // tm
    return pl.pallas_call(
        _mlp_kernel,
        grid=(t // tm, f // tf),
        in_specs=[
            pl.BlockSpec((tm, d), lambda i, j: (blk0 + i, 0)),
            pl.BlockSpec((None, 1, d), lambda i, j: (layer, 0, 0)),
            pl.BlockSpec((d, tf), lambda i, j: (0, j)),
            pl.BlockSpec((tf, d), lambda i, j: (j, 0)),
            pl.BlockSpec((None, 1, d), lambda i, j: (layer, 0, 0)),
        ],
        out_specs=pl.BlockSpec((tm, d), lambda i, j: (i, 0)),
        out_shape=jax.ShapeDtypeStruct((t, d), F32),
        scratch_shapes=[pltpu.VMEM((tm, d), BF16)],
        compiler_params=_params(("arbitrary", "arbitrary")),
        name="mlp",
    )(x, g_pre, w1, w2, g_post)


def _cast_kernel(w_ref, o_ref):
    o_ref[...] = w_ref[...].astype(o_ref.dtype)


def _regroup_w_in(w_in, layer):
    _, d, n = w_in.shape
    cb = 2 * KV_WIDTH
    assert ATTN_WIDTH % cb == 0 and n % cb == 0
    q_blocks = ATTN_WIDTH // cb
    n_main = n - cb
    return pl.pallas_call(
        _cast_kernel,
        grid=(n_main // cb,),
        in_specs=[pl.BlockSpec((None, d, cb),
                               lambda j: (layer, 0, jnp.where(j < q_blocks, j, j + 1)))],
        out_specs=pl.BlockSpec((d, cb), lambda j: (0, j)),
        out_shape=jax.ShapeDtypeStruct((d, n_main), BF16),
        compiler_params=_params(("arbitrary",)),
        name="regroup_w_in",
    )(w_in)


def _tile_edges(seq_lens, tile):
    prev, nxt = [], []
    for s in seq_lens:
        assert s % tile == 0
        n = s // tile
        prev += [0] + [1] * (n - 1)
        nxt += [1] * (n - 1) + [0]
    return jnp.asarray(np.array([prev, nxt], dtype=np.int32))


def _rope_tables(seq_lens):
    max_len = max(seq_lens)
    inv_freq = ROPE_THETA ** (-jnp.arange(0, ROT_DIM, 2, dtype=F32) / ROT_DIM)
    ang = jnp.arange(max_len).astype(F32)[:, None] * inv_freq[None, :]
    cos, sin = jnp.cos(ang), jnp.sin(ang)
    rest = HEAD_DIM - ROT_DIM
    cos_t = jnp.concatenate([cos, cos, jnp.ones((max_len, rest), F32)], axis=-1)
    sin_t = jnp.concatenate([-sin, sin, jnp.zeros((max_len, rest), F32)], axis=-1)
    per_row = lambda tab: jnp.concatenate([tab[:n] for n in seq_lens], axis=0)
    return per_row(cos_t), per_row(sin_t)


def _trunk(groups, params, tiles):
    (g_pre_mix, w_in, b_gate, w_sink, w_conv, w_attn_out, w_conv_out, w_out,
     g_post_mix, g_pre_mlp, w_mlp_in, w_mlp_out, g_post_mlp) = params
    depth, d, _ = w_in.shape
    seq_lens = [g.shape[1] for g in groups for _ in range(g.shape[0])]
    x = jnp.concatenate([g.reshape(-1, d) for g in groups], axis=0)

    cos_rows, sin_rows = _rope_tables(seq_lens)
    mix_edges = _tile_edges(seq_lens, tiles.mix_rows)

    kv_cols = slice(ATTN_WIDTH, ATTN_WIDTH + 2 * KV_WIDTH)
    w_main = _regroup_w_in(w_in, 0)
    w_kv = w_in[0, :, kv_cols].astype(BF16)
    w_ao, w_co, w_o = (w[0].astype(BF16) for w in (w_attn_out, w_conv_out, w_out))
    row = lambda p: p[:, None, :]
    n_other = w_in.shape[2] - 2 * KV_WIDTH - b_gate.shape[1]
    bias = row(jnp.concatenate([jnp.zeros((depth, n_other), F32), b_gate], axis=1))
    mlp = lambda l, xin, w1, w2, tm, tf, **kw: _mlp(l, xin, row(g_pre_mlp), w1, w2,
                                                    row(g_post_mlp), tm, tf, **kw)

    for l in range(depth):
        has_next = l + 1 < depth
        cast_plain = [(w_mlp_in, l), (w_mlp_out, l)]
        if has_next:
            cast_plain += [(w_attn_out, l + 1), (w_conv_out, l + 1), (w_out, l + 1)]
        proj, kv, w1, w2, *nxt = _in_proj(
            l, x, row(g_pre_mix), w_main, w_kv, cos_rows, sin_rows, bias, cast_plain,
            (w_in, l + 1) if has_next else None, tiles)
        x = _mixer(l, x, proj, kv, w_sink, w_conv, w_ao, w_co, w_o,
                   row(g_post_mix), mix_edges, tiles)
        if has_next:
            x = mlp(l, x, w1, w2, tiles.mlp_rows, tiles.mlp_cols)
            w_ao, w_co, w_o, w_main, w_kv = nxt

    outs, r0 = [], 0
    for g in groups:
        n = g.shape[0] * g.shape[1]
        outs.append(mlp(depth - 1, x, w1, w2, tiles.out_rows, tiles.out_cols,
                        row0=r0, rows=n).reshape(g.shape))
        r0 += n
    return tuple(outs)


def kernel(x_prompt, x_sample, g_pre_mix, w_in, b_gate, w_sink, w_conv, w_attn_out, w_conv_out,
           w_out, g_post_mix, g_pre_mlp, w_mlp_in, w_mlp_out, g_post_mlp):
    params = (g_pre_mix, w_in, b_gate, w_sink, w_conv, w_attn_out, w_conv_out, w_out,
              g_post_mix, g_pre_mlp, w_mlp_in, w_mlp_out, g_post_mlp)
    return _trunk([x_prompt, x_sample], params, V7X_TILES)
```

```python
import functools
from typing import NamedTuple

import numpy as np
import jax
import jax.numpy as jnp
from jax import lax
from jax.experimental import pallas as pl
from jax.experimental.pallas import tpu as pltpu

F32 = jnp.float32
BF16 = jnp.bfloat16

HEAD_DIM = 128
N_Q_HEADS = 8
N_KV_HEADS = 2
GROUP = N_Q_HEADS // N_KV_HEADS
ATTN_WIDTH = N_Q_HEADS * HEAD_DIM
KV_WIDTH = N_KV_HEADS * HEAD_DIM
BLOCK = 128
ROT_DIM = HEAD_DIM // 4
ROT_HALF = ROT_DIM // 2
ROPE_THETA = 500000.0
RMS_EPS = 1e-6
NEG_INF = -1e30
LOG2_E = 1.4426950408889634
QK_SCALE = HEAD_DIM ** -0.5 * LOG2_E
SUBLANES = 8
BF16_SUBLANES = 16
V7X_VMEM_LIMIT_BYTES = 60000 * 1024


class Tiles(NamedTuple):
    proj_rows: int
    proj_cols: int
    mix_rows: int
    mlp_rows: int
    mlp_cols: int
    out_rows: int
    out_cols: int


V7X_TILES = Tiles(proj_rows=768, proj_cols=2048, mix_rows=512, mlp_rows=512, mlp_cols=2048,
                  out_rows=512, out_cols=2048)


def _rmsnorm(x, g):
    return x * lax.rsqrt(jnp.mean(x * x, axis=-1, keepdims=True) + RMS_EPS) * g


def _rope_head(xh, cos, sin, lane):
    partner = jnp.where(lane < ROT_HALF,
                        pltpu.roll(xh, HEAD_DIM - ROT_HALF, 1),
                        pltpu.roll(xh, ROT_HALF, 1))
    return jnp.where(lane < ROT_DIM, xh * cos + partner * sin, xh)


def _params(semantics):
    return pltpu.CompilerParams(dimension_semantics=semantics,
                                vmem_limit_bytes=V7X_VMEM_LIMIT_BYTES)


def _in_proj_kernel(n_plain, regroup_next, x_ref, g_ref, w_ref, wkv_ref, cos_ref, sin_ref,
                    bias_ref, *refs):
    n_src = n_plain + (1 if regroup_next else 0)
    cast_src = refs[:n_src]
    proj_ref, kv_ref = refs[n_src:n_src + 2]
    cast_dst = refs[n_src + 2:-1]
    h_ref = refs[-1]
    j = pl.program_id(1)
    rows = x_ref.shape[0]
    lane = lax.broadcasted_iota(jnp.int32, (rows, HEAD_DIM), 1)

    def cast_slices():
        for src, dst in zip(cast_src[:n_plain], cast_dst[:n_plain]):
            dst[...] = src[...].astype(BF16)
        if regroup_next:
            w = cast_src[n_plain][...]
            main_ref, kv_next_ref = cast_dst[n_plain:]
            kv_end = ATTN_WIDTH + 2 * KV_WIDTH
            main_ref[...] = jnp.concatenate([w[:, :ATTN_WIDTH], w[:, kv_end:]],
                                            axis=1).astype(BF16)
            kv_next_ref[...] = w[:, ATTN_WIDTH:kv_end].astype(BF16)

    @pl.when(j == 0)
    def _():
        h_ref[...] = _rmsnorm(x_ref[...], g_ref[...]).astype(BF16)
        cos = cos_ref[...]
        sin = sin_ref[...]
        kv = jnp.dot(h_ref[...], wkv_ref[...], preferred_element_type=F32)
        for hd in range(N_KV_HEADS):
            sl = slice(hd * HEAD_DIM, (hd + 1) * HEAD_DIM)
            kv_ref[:, sl] = _rope_head(kv[:, sl], cos, sin, lane).astype(BF16)
        kv_ref[:, KV_WIDTH:] = kv[:, KV_WIDTH:].astype(BF16)
        q = jnp.dot(h_ref[...], w_ref[...], preferred_element_type=F32)
        for hd in range(N_Q_HEADS):
            sl = slice(hd * HEAD_DIM, (hd + 1) * HEAD_DIM)
            proj_ref[:, sl] = (_rope_head(q[:, sl], cos, sin, lane) * QK_SCALE).astype(BF16)
        if q.shape[1] > ATTN_WIDTH:
            proj_ref[:, ATTN_WIDTH:] = q[:, ATTN_WIDTH:].astype(BF16)
        cast_slices()

    @pl.when(j != 0)
    def _():
        proj_ref[...] = (jnp.dot(h_ref[...], w_ref[...], preferred_element_type=F32)
                         + bias_ref[...]).astype(BF16)
        cast_slices()


def _in_proj(layer, x, g, w_main, w_kv, cos_rows, sin_rows, bias, cast_plain, w_in_next, tiles):
    t, d = x.shape
    n_main = w_main.shape[1]
    tm, tn = tiles.proj_rows, tiles.proj_cols
    assert tn % ATTN_WIDTH == 0 and n_main % tn == 0 and t % tm == 0
    nj = n_main // tn
    steps = (t // tm) * nj
    step = lambda i, j: i * nj + j

    def slice_specs(w, lyr, out_cols):
        rows, cols = w.shape[1:]
        r = max(BF16_SUBLANES, rows // steps)
        assert rows % r == 0
        last = rows // r - 1
        chunk = lambda i, j: jnp.minimum(step(i, j), last)
        src = pl.BlockSpec((None, r, cols), lambda i, j: (lyr, chunk(i, j), 0))
        dst = [pl.BlockSpec((r, c), lambda i, j: (chunk(i, j), 0)) for c in out_cols]
        shapes = [jax.ShapeDtypeStruct((rows, c), BF16) for c in out_cols]
        return src, dst, shapes

    cast_in, cast_out, cast_shapes, cast_args = [], [], [], []
    for w, lyr in cast_plain:
        src, dst, shapes = slice_specs(w, lyr, [w.shape[2]])
        cast_in.append(src); cast_out += dst; cast_shapes += shapes; cast_args.append(w)
    if w_in_next is not None:
        w, lyr = w_in_next
        kv_cols = 2 * KV_WIDTH
        src, dst, shapes = slice_specs(w, lyr, [w.shape[2] - kv_cols, kv_cols])
        cast_in.append(src); cast_out += dst; cast_shapes += shapes; cast_args.append(w)

    return pl.pallas_call(
        functools.partial(_in_proj_kernel, len(cast_plain), w_in_next is not None),
        grid=(t // tm, nj),
        in_specs=[
            pl.BlockSpec((tm, d), lambda i, j: (i, 0)),
            pl.BlockSpec((None, 1, d), lambda i, j: (layer, 0, 0)),
            pl.BlockSpec((d, tn), lambda i, j: (0, j)),
            pl.BlockSpec((d, 2 * KV_WIDTH), lambda i, j: (0, 0), pipeline_mode=pl.Buffered(1)),
            pl.BlockSpec((tm, HEAD_DIM), lambda i, j: (i, 0)),
            pl.BlockSpec((tm, HEAD_DIM), lambda i, j: (i, 0)),
            pl.BlockSpec((None, 1, tn), lambda i, j: (layer, 0, j)),
        ] + cast_in,
        out_specs=[
            pl.BlockSpec((tm, tn), lambda i, j: (i, j)),
            pl.BlockSpec((tm, 2 * KV_WIDTH), lambda i, j: (i, 0)),
        ] + cast_out,
        scratch_shapes=[pltpu.VMEM((tm, d), BF16)],
        out_shape=[jax.ShapeDtypeStruct((t, n_main), BF16),
                   jax.ShapeDtypeStruct((t, 2 * KV_WIDTH), BF16)] + cast_shapes,
        compiler_params=_params(("arbitrary", "arbitrary")),
        name="in_proj",
    )(x, g, w_main, w_kv, cos_rows, sin_rows, bias, *cast_args)


def _attention_tile(layer, has_prev, has_next, sink_ref, q_ref, k_ref, v_ref,
                    kp_ref, vp_ref, kn_ref, vn_ref, attn_ref):
    rows = q_ref.shape[0]
    nblk = rows // BLOCK
    r_io = lax.broadcasted_iota(jnp.int32, (BLOCK, BLOCK), 0)
    s_io = lax.broadcasted_iota(jnp.int32, (BLOCK, BLOCK), 1)
    tri_prev = s_io >= r_io
    tri_next = s_io <= r_io

    def window(ref, halo_prev, halo_next, b, csl):
        prev = halo_prev[:, csl] if b == 0 else ref[(b - 1) * BLOCK:b * BLOCK, csl]
        nxt = halo_next[:, csl] if b == nblk - 1 else ref[(b + 1) * BLOCK:(b + 2) * BLOCK, csl]
        return jnp.concatenate([prev, ref[b * BLOCK:(b + 1) * BLOCK, csl], nxt], axis=0)

    for b in range(nblk):
        valid_prev = tri_prev if b > 0 else jnp.logical_and(tri_prev, has_prev)
        valid_next = tri_next if b < nblk - 1 else jnp.logical_and(tri_next, has_next)
        rsl = slice(b * BLOCK, (b + 1) * BLOCK)
        for c in range(N_KV_HEADS):
            csl = slice(c * HEAD_DIM, (c + 1) * HEAD_DIM)
            kb = window(k_ref, kp_ref, kn_ref, b, csl)
            vb = window(v_ref, vp_ref, vn_ref, b, csl)
            heads = [c * GROUP + g for g in range(GROUP)]
            qs = jnp.concatenate(
                [q_ref[rsl, h * HEAD_DIM:(h + 1) * HEAD_DIM] for h in heads], axis=0)
            s = lax.dot_general(qs, kb, (((1,), (1,)), ((), ())), preferred_element_type=F32)
            ps, invs = [], []
            for g, h in enumerate(heads):
                sink = sink_ref[layer, h] * LOG2_E
                sg = s[g * BLOCK:(g + 1) * BLOCK]
                sg = jnp.concatenate(
                    [jnp.where(valid_prev, sg[:, :BLOCK], NEG_INF),
                     sg[:, BLOCK:2 * BLOCK],
                     jnp.where(valid_next, sg[:, 2 * BLOCK:], NEG_INF)], axis=1)
                m = jnp.maximum(jnp.max(sg, axis=-1, keepdims=True), sink)
                p = jnp.exp2(sg - m)
                den = jnp.sum(p, axis=-1, keepdims=True) + jnp.exp2(sink - m)
                ps.append(p.astype(BF16))
                invs.append(1.0 / den)
            o = jnp.dot(jnp.concatenate(ps, axis=0), vb, preferred_element_type=F32)
            for g, h in enumerate(heads):
                attn_ref[rsl, h * HEAD_DIM:(h + 1) * HEAD_DIM] = (
                    o[g * BLOCK:(g + 1) * BLOCK] * invs[g]).astype(BF16)


def _short_conv_tile(has_prev, has_next, b_ref, c_ref, xc_ref, cp_ref, xp_ref, cn_ref, xn_ref,
                     wconv_ref):
    rows = c_ref.shape[0]
    u = c_ref[...].astype(F32) * xc_ref[...].astype(F32)
    up = (cp_ref[SUBLANES - 1:SUBLANES, :].astype(F32)
          * xp_ref[SUBLANES - 1:SUBLANES, :].astype(F32))
    un = cn_ref[0:1, :].astype(F32) * xn_ref[0:1, :].astype(F32)
    up = jnp.where(has_prev, up, 0.0)
    un = jnp.where(has_next, un, 0.0)
    r_io = lax.broadcasted_iota(jnp.int32, u.shape, 0)
    u_prev = jnp.where(r_io == 0, up, pltpu.roll(u, 1, 0))
    u_next = jnp.where(r_io == rows - 1, un, pltpu.roll(u, rows - 1, 0))
    conv = u_prev * wconv_ref[0:1, :] + u * wconv_ref[1:2, :] + u_next * wconv_ref[2:3, :]
    return (b_ref[...].astype(F32) * conv).astype(BF16)


def _mixer_kernel(layer, edge_ref, sink_ref, q_ref, k_ref, v_ref, kp_ref, vp_ref, kn_ref, vn_ref,
                  b_ref, c_ref, xc_ref, cp_ref, xp_ref, cn_ref, xn_ref,
                  ga_ref, gc_ref, wconv_ref, wao_ref, wco_ref, wout_ref,
                  x_ref, gpost_ref, o_ref, attn_ref):
    i = pl.program_id(0)
    has_prev = edge_ref[0, i] == 1
    has_next = edge_ref[1, i] == 1

    _attention_tile(layer, has_prev, has_next, sink_ref, q_ref, k_ref, v_ref,
                    kp_ref, vp_ref, kn_ref, vn_ref, attn_ref)
    conv = _short_conv_tile(has_prev, has_next, b_ref, c_ref, xc_ref, cp_ref, xp_ref,
                            cn_ref, xn_ref, wconv_ref)

    o_a = jnp.dot(attn_ref[...], wao_ref[...], preferred_element_type=F32)
    o_c = jnp.dot(conv, wco_ref[...], preferred_element_type=F32)
    merged = (jax.nn.sigmoid(ga_ref[...].astype(F32)) * o_a
              + jax.nn.sigmoid(gc_ref[...].astype(F32)) * o_c)
    y = jnp.dot(merged.astype(BF16), wout_ref[...], preferred_element_type=F32)
    o_ref[...] = x_ref[...] + _rmsnorm(y, gpost_ref[...])


def _mixer(layer, x, proj, kv, sink, w_conv, w_ao, w_co, w_out, g_post, edges, tiles):
    t, d = x.shape
    cw = w_co.shape[0]
    tm = tiles.mix_rows
    assert t % tm == 0 and tm % BLOCK == 0
    assert ATTN_WIDTH == cw and (ATTN_WIDTH + 3 * cw) % d == 0
    ga_blk = (ATTN_WIDTH + 3 * cw) // d
    per8, last8 = tm // SUBLANES, t // SUBLANES - 1
    perb, lastb = tm // BLOCK, t // BLOCK - 1
    row_before = lambda per, col: (lambda i, e: (jnp.maximum(i * per - 1, 0), col))
    row_after = lambda per, last, col: (lambda i, e: (jnp.minimum((i + 1) * per, last), col))
    resident = pl.Buffered(1)
    grid_spec = pltpu.PrefetchScalarGridSpec(
        num_scalar_prefetch=1,
        grid=(t // tm,),
        in_specs=[
            pl.BlockSpec(memory_space=pltpu.SMEM),
            pl.BlockSpec((tm, ATTN_WIDTH), lambda i, e: (i, 0)),
            pl.BlockSpec((tm, KV_WIDTH), lambda i, e: (i, 0)),
            pl.BlockSpec((tm, KV_WIDTH), lambda i, e: (i, 1)),
            pl.BlockSpec((BLOCK, KV_WIDTH), row_before(perb, 0)),
            pl.BlockSpec((BLOCK, KV_WIDTH), row_before(perb, 1)),
            pl.BlockSpec((BLOCK, KV_WIDTH), row_after(perb, lastb, 0)),
            pl.BlockSpec((BLOCK, KV_WIDTH), row_after(perb, lastb, 1)),
            pl.BlockSpec((tm, cw), lambda i, e: (i, 1)),
            pl.BlockSpec((tm, cw), lambda i, e: (i, 2)),
            pl.BlockSpec((tm, cw), lambda i, e: (i, 3)),
            pl.BlockSpec((SUBLANES, cw), row_before(per8, 2)),
            pl.BlockSpec((SUBLANES, cw), row_before(per8, 3)),
            pl.BlockSpec((SUBLANES, cw), row_after(per8, last8, 2)),
            pl.BlockSpec((SUBLANES, cw), row_after(per8, last8, 3)),
            pl.BlockSpec((tm, d), lambda i, e: (i, ga_blk)),
            pl.BlockSpec((tm, d), lambda i, e: (i, ga_blk + 1)),
            pl.BlockSpec((None, 3, cw), lambda i, e: (layer, 0, 0)),
            pl.BlockSpec((ATTN_WIDTH, d), lambda i, e: (0, 0), pipeline_mode=resident),
            pl.BlockSpec((cw, d), lambda i, e: (0, 0), pipeline_mode=resident),
            pl.BlockSpec((d, d), lambda i, e: (0, 0), pipeline_mode=resident),
            pl.BlockSpec((tm, d), lambda i, e: (i, 0)),
            pl.BlockSpec((None, 1, d), lambda i, e: (layer, 0, 0)),
        ],
        out_specs=pl.BlockSpec((tm, d), lambda i, e: (i, 0)),
        scratch_shapes=[pltpu.VMEM((tm, ATTN_WIDTH), BF16)],
    )
    return pl.pallas_call(
        functools.partial(_mixer_kernel, layer),
        grid_spec=grid_spec,
        out_shape=jax.ShapeDtypeStruct((t, d), F32),
        compiler_params=_params(("arbitrary",)),
        name="mixer",
    )(edges, sink, proj, kv, kv, kv, kv, kv, kv, proj, proj, proj, proj, proj, proj, proj,
      proj, proj, w_conv, w_ao, w_co, w_out, x, g_post)


def _mlp_kernel(x_ref, gpre_ref, w1_ref, w2_ref, gpost_ref, o_ref, h_ref):
    j = pl.program_id(1)
    last = pl.num_programs(1) - 1

    def hidden_chunk():
        a = jnp.dot(h_ref[...], w1_ref[...], preferred_element_type=F32)
        a = jnp.square(jnp.maximum(a, 0.0)).astype(BF16)
        return jnp.dot(a, w2_ref[...], preferred_element_type=F32)

    @pl.when(j == 0)
    def _():
        h_ref[...] = _rmsnorm(x_ref[...], gpre_ref[...]).astype(BF16)
        o_ref[...] = hidden_chunk()

    @pl.when(jnp.logical_and(j > 0, j < last))
    def _():
        o_ref[...] += hidden_chunk()

    @pl.when(j == last)
    def _():
        f = o_ref[...] + hidden_chunk()
        o_ref[...] = x_ref[...] + _rmsnorm(f, gpost_ref[...])


def _mlp(layer, x, g_pre, w1, w2, g_post, tm, tf, row0=0, rows=None):
    d = x.shape[1]
    t = x.shape[0] if rows is None else rows
    f = w1.shape[1]
    assert t % tm == 0 and f % tf == 0 and row0 % tm == 0
    blk0 = row0 // tm
    return pl.pallas_call(
        _mlp_kernel,
        grid=(t // tm, f // tf),
        in_specs=[
            pl.BlockSpec((tm, d), lambda i, j: (blk0 + i, 0)),
            pl.BlockSpec((None, 1, d), lambda i, j: (layer, 0, 0)),
            pl.BlockSpec((d, tf), lambda i, j: (0, j)),
            pl.BlockSpec((tf, d), lambda i, j: (j, 0)),
            pl.BlockSpec((None, 1, d), lambda i, j: (layer, 0, 0)),
        ],
        out_specs=pl.BlockSpec((tm, d), lambda i, j: (i, 0)),
        out_shape=jax.ShapeDtypeStruct((t, d), F32),
        scratch_shapes=[pltpu.VMEM((tm, d), BF16)],
        compiler_params=_params(("arbitrary", "arbitrary")),
        name="mlp",
    )(x, g_pre, w1, w2, g_post)


def _cast_kernel(w_ref, o_ref):
    o_ref[...] = w_ref[...].astype(o_ref.dtype)


def _regroup_w_in(w_in, layer):
    _, d, n = w_in.shape
    cb = 2 * KV_WIDTH
    assert ATTN_WIDTH % cb == 0 and n % cb == 0
    q_blocks = ATTN_WIDTH // cb
    n_main = n - cb
    return pl.pallas_call(
        _cast_kernel,
        grid=(n_main // cb,),
        in_specs=[pl.BlockSpec((None, d, cb),
                               lambda j: (layer, 0, jnp.where(j < q_blocks, j, j + 1)))],
        out_specs=pl.BlockSpec((d, cb), lambda j: (0, j)),
        out_shape=jax.ShapeDtypeStruct((d, n_main), BF16),
        compiler_params=_params(("arbitrary",)),
        name="regroup_w_in",
    )(w_in)


def _tile_edges(seq_lens, tile):
    prev, nxt = [], []
    for s in seq_lens:
        assert s % tile == 0
        n = s // tile
        prev += [0] + [1] * (n - 1)
        nxt += [1] * (n - 1) + [0]
    return jnp.asarray(np.array([prev, nxt], dtype=np.int32))


def _rope_tables(seq_lens):
    max_len = max(seq_lens)
    inv_freq = ROPE_THETA ** (-jnp.arange(0, ROT_DIM, 2, dtype=F32) / ROT_DIM)
    ang = jnp.arange(max_len).astype(F32)[:, None] * inv_freq[None, :]
    cos, sin = jnp.cos(ang), jnp.sin(ang)
    rest = HEAD_DIM - ROT_DIM
    cos_t = jnp.concatenate([cos, cos, jnp.ones((max_len, rest), F32)], axis=-1)
    sin_t = jnp.concatenate([-sin, sin, jnp.zeros((max_len, rest), F32)], axis=-1)
    per_row = lambda tab: jnp.concatenate([tab[:n] for n in seq_lens], axis=0)
    return per_row(cos_t), per_row(sin_t)


def _trunk(groups, params, tiles):
    (g_pre_mix, w_in, b_gate, w_sink, w_conv, w_attn_out, w_conv_out, w_out,
     g_post_mix, g_pre_mlp, w_mlp_in, w_mlp_out, g_post_mlp) = params
    depth, d, _ = w_in.shape
    seq_lens = [g.shape[1] for g in groups for _ in range(g.shape[0])]
    x = jnp.concatenate([g.reshape(-1, d) for g in groups], axis=0)

    cos_rows, sin_rows = _rope_tables(seq_lens)
    mix_edges = _tile_edges(seq_lens, tiles.mix_rows)

    kv_cols = slice(ATTN_WIDTH, ATTN_WIDTH + 2 * KV_WIDTH)
    w_main = _regroup_w_in(w_in, 0)
    w_kv = w_in[0, :, kv_cols].astype(BF16)
    w_ao, w_co, w_o = (w[0].astype(BF16) for w in (w_attn_out, w_conv_out, w_out))
    row = lambda p: p[:, None, :]
    n_other = w_in.shape[2] - 2 * KV_WIDTH - b_gate.shape[1]
    bias = row(jnp.concatenate([jnp.zeros((depth, n_other), F32), b_gate], axis=1))
    mlp = lambda l, xin, w1, w2, tm, tf, **kw: _mlp(l, xin, row(g_pre_mlp), w1, w2,
                                                    row(g_post_mlp), tm, tf, **kw)

    for l in range(depth):
        has_next = l + 1 < depth
        cast_plain = [(w_mlp_in, l), (w_mlp_out, l)]
        if has_next:
            cast_plain += [(w_attn_out, l + 1), (w_conv_out, l + 1), (w_out, l + 1)]
        proj, kv, w1, w2, *nxt = _in_proj(
            l, x, row(g_pre_mix), w_main, w_kv, cos_rows, sin_rows, bias, cast_plain,
            (w_in, l + 1) if has_next else None, tiles)
        x = _mixer(l, x, proj, kv, w_sink, w_conv, w_ao, w_co, w_o,
                   row(g_post_mix), mix_edges, tiles)
        if has_next:
            x = mlp(l, x, w1, w2, tiles.mlp_rows, tiles.mlp_cols)
            w_ao, w_co, w_o, w_main, w_kv = nxt

    outs, r0 = [], 0
    for g in groups:
        n = g.shape[0] * g.shape[1]
        outs.append(mlp(depth - 1, x, w1, w2, tiles.out_rows, tiles.out_cols,
                        row0=r0, rows=n).reshape(g.shape))
        r0 += n
    return tuple(outs)


def kernel(x_prompt, x_sample, g_pre_mix, w_in, b_gate, w_sink, w_conv, w_attn_out, w_conv_out,
           w_out, g_post_mix, g_pre_mlp, w_mlp_in, w_mlp_out, g_post_mlp):
    params = (g_pre_mix, w_in, b_gate, w_sink, w_conv, w_attn_out, w_conv_out, w_out,
              g_post_mix, g_pre_mlp, w_mlp_in, w_mlp_out, g_post_mlp)
    return _trunk([x_prompt, x_sample], params, V7X_TILES)
```
